```python
import math
import jax, jax.numpy as jnp
from jax import lax
import numpy as np

D_MODEL = 1024
BATCH = 16
SEQ = 256
DEPTH = 4
DEC_BATCH = 2
DEC_SEQ = 4096
PAST_LEN = 512

GRID_W = 64
N_MIXERS = 4
N_A = (DEPTH + 3) // N_MIXERS
N_B = (DEPTH + 2) // N_MIXERS
N_C = (DEPTH + 1) // N_MIXERS
N_D = DEPTH // N_MIXERS
D_FF = 2816
MACARON_W = 0.5
RMS_EPS = 1e-6
NEG_INF = -1e30
Q_BLOCK = 128
A_HEADS = 16
A_KV_HEADS = 4
A_HEAD_DIM = D_MODEL // A_HEADS
A_WINDOW = 128
A_BLOCK = 128
ROPE_BASE = 10000.0
S5_GROUP = 16
S5_GROUPS = D_MODEL // S5_GROUP
S5_STATE = 64
C_HEADS = 16
C_HEAD_DIM = D_MODEL // C_HEADS
NA_ROWS = 8
NA_COLS = 16
DN_QK_HEADS = 4
DN_V_HEADS = 8
DN_HEAD_DIM = 128
DN_CONV = 5
DN_CHUNK = 64

kernel_name = 'hybrid_diffusion_prefix_trunk_step'

F32 = jnp.float32


def rmsnorm(x, g):
    xf = x.astype(F32)
    y = xf * lax.rsqrt(jnp.mean(xf * xf, axis=-1, keepdims=True) + RMS_EPS)
    return (y * g.astype(F32)).astype(x.dtype)


def l2norm(x):
    xf = x.astype(F32)
    return (xf * lax.rsqrt(jnp.sum(xf * xf, axis=-1, keepdims=True) + RMS_EPS)).astype(x.dtype)


def swiglu(h, w_gu, w_d):
    g, u = jnp.split(h @ w_gu, 2, axis=-1)
    return (jax.nn.silu(g) * u) @ w_d


def rotate(x, ang):
    ang = ang.reshape((ang.shape[0],) + (1,) * (x.ndim - 3) + (ang.shape[1],))
    cos, sin = jnp.cos(ang).astype(x.dtype), jnp.sin(ang).astype(x.dtype)
    x1, x2 = jnp.split(x, 2, axis=-1)
    return jnp.concatenate([x1 * cos - x2 * sin, x2 * cos + x1 * sin], axis=-1)


def axial_rope(x):
    L, d = x.shape[1], x.shape[-1]
    n = d // 4
    inv = ROPE_BASE ** (-jnp.arange(n, dtype=F32) / n)
    t = jnp.arange(L)
    ang_r = (t // GRID_W).astype(F32)[:, None] * inv[None, :]
    ang_c = (t % GRID_W).astype(F32)[:, None] * inv[None, :]
    half = d // 2
    return jnp.concatenate([rotate(x[..., :half], ang_r), rotate(x[..., half:], ang_c)], axis=-1)


def blocked_attention(q, k, v, sink):
    B, Lq = q.shape[:2]
    nb = Lq // Q_BLOCK
    scale = q.shape[-1] ** -0.5
    qb = jnp.moveaxis(q.reshape((B, nb, Q_BLOCK) + q.shape[2:]), 1, 0)

    def one_block(qblk):
        s = jnp.einsum('bikgd,bjkd->bkgij', qblk, k).astype(F32) * scale
        if sink is not None:
            sk = jnp.broadcast_to(sink.astype(F32)[None, :, :, None, None], s.shape[:-1] + (1,))
            p = jax.nn.softmax(jnp.concatenate([s, sk], axis=-1), axis=-1)[..., :-1]
        else:
            p = jax.nn.softmax(s, axis=-1)
        return jnp.einsum('bkgij,bjkd->bikgd', p.astype(v.dtype), v)

    o = lax.map(one_block, qb)
    return jnp.moveaxis(o, 0, 1).reshape(q.shape)


def gqa_project(h, w_qkv):
    B, L, _ = h.shape
    G = A_HEADS // A_KV_HEADS
    q, k, v = jnp.split(h @ w_qkv, [A_HEADS * A_HEAD_DIM, (A_HEADS + A_KV_HEADS) * A_HEAD_DIM], axis=-1)
    return (q.reshape(B, L, A_KV_HEADS, G, A_HEAD_DIM), k.reshape(B, L, A_KV_HEADS, A_HEAD_DIM),
            v.reshape(B, L, A_KV_HEADS, A_HEAD_DIM))


def mixer_a_ctx(h, w_qkv, w_o, sink):
    B, L, _ = h.shape
    q, k, v = gqa_project(h, w_qkv)
    o = blocked_attention(q, k, v, sink.reshape(A_KV_HEADS, A_HEADS // A_KV_HEADS))
    return o.reshape(B, L, -1) @ w_o, (k, v)


def mixer_a_lat(h, w_qkv, w_o, sink, k_ctx, v_ctx):
    B, L, _ = h.shape
    G = A_HEADS // A_KV_HEADS
    q, k, v = gqa_project(h, w_qkv)
    q, k = axial_rope(q), axial_rope(k)
    nb = L // A_BLOCK
    side = A_WINDOW // A_BLOCK
    width = (2 * side + 1) * A_BLOCK

    def band(t):
        tp = jnp.pad(t, ((0, 0), (side * A_BLOCK, side * A_BLOCK), (0, 0), (0, 0)))
        tp = tp.reshape((B, nb + 2 * side, A_BLOCK) + t.shape[2:])
        return jnp.concatenate([tp[:, o:o + nb] for o in range(2 * side + 1)], axis=2)

    kb, vb = band(k), band(v)
    qb = q.reshape(B, nb, A_BLOCK, A_KV_HEADS, G, A_HEAD_DIM)
    blk = jnp.arange(nb)[:, None] * A_BLOCK
    q_pos = blk + jnp.arange(A_BLOCK)[None, :]
    k_pos = blk - side * A_BLOCK + jnp.arange(width)[None, :]
    ok = ((jnp.abs(k_pos[:, None, :] - q_pos[:, :, None]) <= A_WINDOW)
          & ((k_pos >= 0) & (k_pos < L))[:, None, :])
    scale = A_HEAD_DIM ** -0.5
    s_loc = jnp.einsum('bnikgd,bnjkd->bnkgij', qb, kb).astype(F32) * scale
    s_loc = jnp.where(ok[None, :, None, None], s_loc, NEG_INF)
    s_ctx = jnp.einsum('bnikgd,bmkd->bnkgim', qb, k_ctx).astype(F32) * scale
    sk = jnp.broadcast_to(sink.astype(F32).reshape(1, 1, A_KV_HEADS, G, 1, 1), s_loc.shape[:-1] + (1,))
    p = jax.nn.softmax(jnp.concatenate([s_loc, s_ctx, sk], axis=-1), axis=-1).astype(v.dtype)
    o = (jnp.einsum('bnkgij,bnjkd->bnikgd', p[..., :width], vb)
         + jnp.einsum('bnkgim,bmkd->bnikgd', p[..., width:-1], v_ctx))
    return o.reshape(B, L, -1) @ w_o, None


def complex_affine_combine(e1, e2):
    a1r, a1i, b1r, b1i = e1
    a2r, a2i, b2r, b2i = e2
    return (a2r * a1r - a2i * a1i, a2r * a1i + a2i * a1r,
            a2r * b1r - a2i * b1i + b2r, a2r * b1i + a2i * b1r + b2i)


def s5_scan(u, lam_re, lam_im, log_dt, b_re, b_im, h0):
    lam_re, lam_im = lam_re.astype(F32), lam_im.astype(F32)
    dt = jnp.exp(log_dt.astype(F32))[:, None]
    lr, li = lam_re * dt, lam_im * dt
    a_re, a_im = jnp.exp(lr) * jnp.cos(li), jnp.exp(lr) * jnp.sin(li)
    den = lam_re * lam_re + lam_im * lam_im
    fr = ((a_re - 1.0) * lam_re + a_im * lam_im) / den
    fi = (a_im * lam_re - (a_re - 1.0) * lam_im) / den
    b_re, b_im = b_re.astype(F32), b_im.astype(F32)
    bb_re = fr[..., None] * b_re - fi[..., None] * b_im
    bb_im = fr[..., None] * b_im + fi[..., None] * b_re
    bu_re = jnp.einsum('blgc,gpc->blgp', u, bb_re)
    bu_im = jnp.einsum('blgc,gpc->blgp', u, bb_im)
    if h0 is not None:
        bu_re = jnp.concatenate([h0[0][:, None], bu_re], axis=1)
        bu_im = jnp.concatenate([h0[1][:, None], bu_im], axis=1)
    A_re, A_im = jnp.broadcast_to(a_re, bu_re.shape), jnp.broadcast_to(a_im, bu_re.shape)
    _, _, xr, xi = lax.associative_scan(complex_affine_combine, (A_re, A_im, bu_re, bu_im), axis=1)
    if h0 is not None:
        xr, xi = xr[:, 1:], xi[:, 1:]
    return xr, xi


def mixer_s5(h, p, state_re=None, state_im=None):
    lam_re, lam_im, log_dt, b_re, b_im, c_re, c_im, d_skip, w_glu = p
    B, L, _ = h.shape
    u = h.astype(F32).reshape(B, L, S5_GROUPS, S5_GROUP)
    y = jnp.zeros_like(u)
    fin_re, fin_im = [], []
    for d in range(2):
        ud = u if d == 0 else jnp.flip(u, 1)
        h0 = None if state_re is None else (state_re[:, d].astype(F32), state_im[:, d].astype(F32))
        xr, xi = s5_scan(ud, lam_re[d], lam_im[d], log_dt[d], b_re[d], b_im[d], h0)
        yd = (jnp.einsum('gcp,blgp->blgc', c_re[d].astype(F32), xr)
              - jnp.einsum('gcp,blgp->blgc', c_im[d].astype(F32), xi))
        y = y + (yd if d == 0 else jnp.flip(yd, 1))
        if state_re is None:
            fin_re.append(xr[:, -1])
            fin_im.append(xi[:, -1])
    y = y.reshape(B, L, D_MODEL).astype(h.dtype) + d_skip * h
    a, gt = jnp.split(jax.nn.gelu(y) @ w_glu, 2, axis=-1)
    out = a * jax.nn.sigmoid(gt)
    aux = (jnp.stack(fin_re, 1), jnp.stack(fin_im, 1)) if state_re is None else None
    return out, aux


def na_project(h, w_qkv):
    B, L, _ = h.shape
    return [t.reshape(B, L, C_HEADS, C_HEAD_DIM) for t in jnp.split(h @ w_qkv, 3, axis=-1)]


def mixer_na_ctx(h, w_qkv, w_o):
    B, L, _ = h.shape
    q, k, v = na_project(h, w_qkv)
    o = blocked_attention(q[:, :, :, None], k, v, None)
    return o.reshape(B, L, -1) @ w_o, (k, v)


def mixer_na_lat(h, w_qkv, w_o, rpb, k_ctx, v_ctx):
    B, L, _ = h.shape
    rows = L // GRID_W
    kh = min(NA_ROWS, rows)
    q, k, v = [t.reshape(B, rows, GRID_W, C_HEADS, C_HEAD_DIM) for t in na_project(h, w_qkv)]
    r = jnp.arange(rows)
    row_idx = jnp.clip(r - kh // 2, 0, rows - kh)[:, None] + jnp.arange(kh)[None, :]
    kr, vr = k[:, row_idx], v[:, row_idx]
    col = jnp.arange(GRID_W)
    col_start = jnp.clip(col - NA_COLS // 2, 0, GRID_W - NA_COLS)
    col_ok = (col[None, :] >= col_start[:, None]) & (col[None, :] < col_start[:, None] + NA_COLS)
    d_row = row_idx - r[:, None]
    d_col = jnp.clip(col[None, :] - col[:, None], 1 - NA_COLS, NA_COLS - 1)
    bias = rpb[:, (d_row + NA_ROWS - 1)[:, None, :, None], (d_col + NA_COLS - 1)[None, :, None, :]]
    scale = C_HEAD_DIM ** -0.5
    s = jnp.einsum('brqhd,brjkhd->bhrqjk', q, kr).astype(F32) * scale + bias[None].astype(F32)
    s = jnp.where(col_ok[:, None, :], s, NEG_INF).reshape(B, C_HEADS, rows, GRID_W, kh * GRID_W)
    s_ctx = jnp.einsum('brqhd,bmhd->bhrqm', q, k_ctx).astype(F32) * scale
    p = jax.nn.softmax(jnp.concatenate([s, s_ctx], axis=-1), axis=-1).astype(v.dtype)
    n_loc = kh * GRID_W
    p_loc = p[..., :n_loc].reshape(B, C_HEADS, rows, GRID_W, kh, GRID_W)
    o = (jnp.einsum('bhrqjk,brjkhd->brqhd', p_loc, vr)
         + jnp.einsum('bhrqm,bmhd->brqhd', p[..., n_loc:], v_ctx))
    return o.reshape(B, L, -1) @ w_o, None


def centred_conv(x, w):
    K, C = w.shape
    return lax.conv_general_dilated(x, w[:, None, :], window_strides=(1,), padding=[(K // 2, K // 2)],
                                    dimension_numbers=('NWC', 'WIO', 'NWC'), feature_group_count=C)


def gated_delta_rule(q, k, v, g, beta, s0):
    B, L, H, dk = k.shape
    dv = v.shape[-1]
    n = L // DN_CHUNK

    def chunk(t):
        t = t.astype(F32).reshape((B, n, DN_CHUNK) + t.shape[2:])
        return jnp.swapaxes(t, 2, 3)

    q, k, v, g, beta = [chunk(t) for t in (q, k, v, g, beta)]
    gc = jnp.cumsum(g, axis=-1)
    idx = jnp.arange(DN_CHUNK)
    lower = idx[:, None] >= idx[None, :]
    strict = idx[:, None] > idx[None, :]
    decay = jnp.exp(jnp.where(lower, gc[..., :, None] - gc[..., None, :], NEG_INF))
    kb = k * beta[..., None]
    lmat = jnp.where(strict, jnp.einsum('bnhid,bnhjd->bnhij', kb, k) * decay, 0.0)
    rhs = jnp.concatenate([v * beta[..., None], kb * jnp.exp(gc)[..., None]], axis=-1)
    sol = lax.linalg.triangular_solve(lmat + jnp.eye(DN_CHUNK, dtype=F32), rhs, left_side=True, lower=True)
    u, w = sol[..., :dv], sol[..., dv:]
    a_qk = jnp.where(lower, jnp.einsum('bnhid,bnhjd->bnhij', q, k) * decay, 0.0)

    def step(S, xs):
        q_c, k_c, u_c, w_c, g_c, a_c = xs
        v_new = u_c - jnp.einsum('bhcd,bhde->bhce', w_c, S)
        o = (jnp.einsum('bhcd,bhde->bhce', q_c * jnp.exp(g_c)[..., None], S)
             + jnp.einsum('bhij,bhje->bhie', a_c, v_new))
        g_last = g_c[..., -1:]
        S = (S * jnp.exp(g_last)[..., None]
             + jnp.einsum('bhcd,bhce->bhde', k_c * jnp.exp(g_last - g_c)[..., None], v_new))
        return S, o

    xs = tuple(jnp.moveaxis(t, 1, 0) for t in (q, k, u, w, gc, a_qk))
    S, o = lax.scan(step, s0.astype(F32), xs)
    o = jnp.swapaxes(jnp.moveaxis(o, 0, 1), 2, 3).reshape(B, L, H, dv)
    return o, S


def mixer_dn(h, p, state=None):
    w_in, conv_w, w_ba, a_log, dt_bias, out_g, w_o = p
    B, L, _ = h.shape
    nqk = DN_QK_HEADS * DN_HEAD_DIM
    nv = DN_V_HEADS * DN_HEAD_DIM
    proj = h @ w_in
    qkv = jax.nn.silu(centred_conv(proj[..., :2 * nqk + nv], conv_w))
    z = proj[..., 2 * nqk + nv:].reshape(B, L, DN_V_HEADS, DN_HEAD_DIM)
    rep = DN_V_HEADS // DN_QK_HEADS
    q = jnp.repeat(l2norm(qkv[..., :nqk].reshape(B, L, DN_QK_HEADS, DN_HEAD_DIM)), rep, axis=2) * DN_HEAD_DIM ** -0.5
    k = jnp.repeat(l2norm(qkv[..., nqk:2 * nqk].reshape(B, L, DN_QK_HEADS, DN_HEAD_DIM)), rep, axis=2)
    v = qkv[..., 2 * nqk:].reshape(B, L, DN_V_HEADS, DN_HEAD_DIM)
    o_sum = jnp.zeros((B, L, DN_V_HEADS, DN_HEAD_DIM), F32)
    finals = []
    for d in range(2):
        b_raw, a_raw = jnp.split((h @ w_ba[d]).astype(F32), 2, axis=-1)
        beta = jax.nn.sigmoid(b_raw)
        g = -jnp.exp(a_log[d].astype(F32)) * jax.nn.softplus(a_raw + dt_bias[d].astype(F32))
        seqs = (q, k, v, g, beta) if d == 0 else tuple(jnp.flip(t, 1) for t in (q, k, v, g, beta))
        s0 = jnp.zeros((B, DN_V_HEADS, DN_HEAD_DIM, DN_HEAD_DIM), F32) if state is None else state[:, d]
        o, S = gated_delta_rule(*seqs, s0)
        o_sum = o_sum + (o if d == 0 else jnp.flip(o, 1))
        if state is None:
            finals.append(S)
    o = rmsnorm(o_sum, out_g) * jax.nn.silu(z.astype(F32))
    out = o.astype(h.dtype).reshape(B, L, -1) @ w_o
    return out, (jnp.stack(finals, 1) if state is None else None)


def trunk_layer(x, cond, i, mix, norm_g, w_ada, b_ada, ffn_w_gu, ffn_w_d):
    mods = jnp.split((jax.nn.silu(cond) @ w_ada[i] + b_ada[i])[:, None, :], 9, axis=-1)

    def sub(x, s, f, weight):
        shift, scale, gate = mods[3 * s], mods[3 * s + 1], mods[3 * s + 2]
        y, aux = f(rmsnorm(x, norm_g[i, 2 * s]) * (1.0 + scale) + shift)
        return x + weight * gate * rmsnorm(y, norm_g[i, 2 * s + 1]), aux

    x, _ = sub(x, 0, lambda t: (swiglu(t, ffn_w_gu[i, 0], ffn_w_d[i, 0]), None), MACARON_W)
    x, aux = sub(x, 1, mix, 1.0)
    x, _ = sub(x, 2, lambda t: (swiglu(t, ffn_w_gu[i, 1], ffn_w_d[i, 1]), None), MACARON_W)
    return x, aux


def setup_inputs(seed: int = 0) -> dict:
    key = jax.random.key(seed)
    ks = iter(jax.random.split(key, 64))

    def nrm(shape, scale=1.0):
        return scale * jax.random.normal(next(ks), shape, F32)

    def unif(shape, lo, hi):
        return jax.random.uniform(next(ks), shape, F32, lo, hi)

    D = D_MODEL
    nqkv = 2 * DN_QK_HEADS * DN_HEAD_DIM + DN_V_HEADS * DN_HEAD_DIM
    s5_shape = (N_B, 2, S5_GROUPS, S5_STATE)
    dn_dt = jnp.exp(unif((N_D, 2, DN_V_HEADS), math.log(1e-3), math.log(1e-1)))
    return {
        'x_prompt': nrm((BATCH, SEQ, D)),
        'x_sample': nrm((DEC_BATCH, DEC_SEQ, D)),
        'cache_attn_k': nrm((DEC_BATCH, N_A, PAST_LEN, A_KV_HEADS, A_HEAD_DIM)),
        'cache_attn_v': nrm((DEC_BATCH, N_A, PAST_LEN, A_KV_HEADS, A_HEAD_DIM)),
        'state_s5_re': nrm((DEC_BATCH, N_B, 2, S5_GROUPS, S5_STATE), 0.1),
        'state_s5_im': nrm((DEC_BATCH, N_B, 2, S5_GROUPS, S5_STATE), 0.1),
        'cache_na_k': nrm((DEC_BATCH, N_C, PAST_LEN, C_HEADS, C_HEAD_DIM)),
        'cache_na_v': nrm((DEC_BATCH, N_C, PAST_LEN, C_HEADS, C_HEAD_DIM)),
        'state_dn': nrm((DEC_BATCH, N_D, 2, DN_V_HEADS, DN_HEAD_DIM, DN_HEAD_DIM), 0.1),
        'c': nrm((DEC_BATCH, D)),
        'c_ctx': nrm((D,)),
        'norm_g': 1.0 + nrm((DEPTH, 6, D), 0.02),
        'w_ada': nrm((DEPTH, D, 9 * D), 0.5 * D ** -0.5),
        'b_ada': nrm((DEPTH, 9 * D), 0.01),
        'ffn_w_gu': nrm((DEPTH, 2, D, 2 * D_FF), D ** -0.5),
        'ffn_w_d': nrm((DEPTH, 2, D_FF, D), D_FF ** -0.5),
        'a_w_qkv': nrm((N_A, D, (A_HEADS + 2 * A_KV_HEADS) * A_HEAD_DIM), D ** -0.5),
        'a_w_o': nrm((N_A, A_HEADS * A_HEAD_DIM, D), (A_HEADS * A_HEAD_DIM) ** -0.5),
        'a_sink': nrm((N_A, A_HEADS)),
        's5_lam_re': -0.5 + nrm(s5_shape, 0.01),
        's5_lam_im': math.pi * jnp.arange(S5_STATE, dtype=F32) + nrm(s5_shape, 0.01),
        's5_log_dt': unif((N_B, 2, S5_GROUPS), math.log(1e-3), math.log(1e-1)),
        's5_b_re': nrm((N_B, 2, S5_GROUPS, S5_STATE, S5_GROUP), (2 * S5_GROUP) ** -0.5),
        's5_b_im': nrm((N_B, 2, S5_GROUPS, S5_STATE, S5_GROUP), (2 * S5_GROUP) ** -0.5),
        's5_c_re': nrm((N_B, 2, S5_GROUPS, S5_GROUP, S5_STATE), 0.5),
        's5_c_im': nrm((N_B, 2, S5_GROUPS, S5_GROUP, S5_STATE), 0.5),
        's5_d': nrm((N_B, D), 0.5),
        's5_w_glu': nrm((N_B, D, 2 * D), D ** -0.5),
        'na_w_qkv': nrm((N_C, D, 3 * C_HEADS * C_HEAD_DIM), D ** -0.5),
        'na_w_o': nrm((N_C, C_HEADS * C_HEAD_DIM, D), (C_HEADS * C_HEAD_DIM) ** -0.5),
        'na_rpb': nrm((N_C, C_HEADS, 2 * NA_ROWS - 1, 2 * NA_COLS - 1), 0.1),
        'dn_w_in': nrm((N_D, D, nqkv + DN_V_HEADS * DN_HEAD_DIM), D ** -0.5),
        'dn_conv_w': nrm((N_D, DN_CONV, nqkv), DN_CONV ** -0.5),
        'dn_w_ba': nrm((N_D, 2, D, 2 * DN_V_HEADS), D ** -0.5),
        'dn_a_log': jnp.log(unif((N_D, 2, DN_V_HEADS), 1.0, 16.0)),
        'dn_dt_bias': dn_dt + jnp.log(-jnp.expm1(-dn_dt)),
        'dn_out_g': 1.0 + nrm((N_D, DN_HEAD_DIM), 0.02),
        'dn_w_o': nrm((N_D, DN_V_HEADS * DN_HEAD_DIM, D), (DN_V_HEADS * DN_HEAD_DIM) ** -0.5),
    }


def reference(x_prompt, x_sample, cache_attn_k, cache_attn_v, state_s5_re, state_s5_im, cache_na_k, cache_na_v,
              state_dn, c, c_ctx, norm_g, w_ada, b_ada, ffn_w_gu, ffn_w_d, a_w_qkv, a_w_o, a_sink,
              s5_lam_re, s5_lam_im, s5_log_dt, s5_b_re, s5_b_im, s5_c_re, s5_c_im, s5_d, s5_w_glu,
              na_w_qkv, na_w_o, na_rpb, dn_w_in, dn_conv_w, dn_w_ba, dn_a_log, dn_dt_bias, dn_out_g, dn_w_o):
    common = (norm_g, w_ada, b_ada, ffn_w_gu, ffn_w_d)

    def s5_params(j):
        return (s5_lam_re[j], s5_lam_im[j], s5_log_dt[j], s5_b_re[j], s5_b_im[j], s5_c_re[j], s5_c_im[j],
                s5_d[j], s5_w_glu[j])

    def dn_params(j):
        return (dn_w_in[j], dn_conv_w[j], dn_w_ba[j], dn_a_log[j], dn_dt_bias[j], dn_out_g[j], dn_w_o[j])

    y = x_prompt
    cond_ctx = c_ctx[None, :]
    ctx_state = ([], [], [], [])
    for i in range(DEPTH):
        kind, j = i % N_MIXERS, i // N_MIXERS
        if kind == 0:
            mix = lambda t: mixer_a_ctx(t, a_w_qkv[j], a_w_o[j], a_sink[j])
        elif kind == 1:
            mix = lambda t: mixer_s5(t, s5_params(j))
        elif kind == 2:
            mix = lambda t: mixer_na_ctx(t, na_w_qkv[j], na_w_o[j])
        else:
            mix = lambda t: mixer_dn(t, dn_params(j))
        y, aux = trunk_layer(y, cond_ctx, i, mix, *common)
        ctx_state[kind].append(aux)
    new_attn_k = jnp.stack([s[0] for s in ctx_state[0]], 1)
    new_attn_v = jnp.stack([s[1] for s in ctx_state[0]], 1)
    new_s5_re = jnp.stack([s[0] for s in ctx_state[1]], 1)
    new_s5_im = jnp.stack([s[1] for s in ctx_state[1]], 1)
    new_na_k = jnp.stack([s[0] for s in ctx_state[2]], 1)
    new_na_v = jnp.stack([s[1] for s in ctx_state[2]], 1)
    new_dn = jnp.stack(ctx_state[3], 1)

    z = x_sample
    for i in range(DEPTH):
        kind, j = i % N_MIXERS, i // N_MIXERS
        if kind == 0:
            mix = lambda t: mixer_a_lat(t, a_w_qkv[j], a_w_o[j], a_sink[j], cache_attn_k[:, j], cache_attn_v[:, j])
        elif kind == 1:
            mix = lambda t: mixer_s5(t, s5_params(j), state_s5_re[:, j], state_s5_im[:, j])
        elif kind == 2:
            mix = lambda t: mixer_na_lat(t, na_w_qkv[j], na_w_o[j], na_rpb[j], cache_na_k[:, j], cache_na_v[:, j])
        else:
            mix = lambda t: mixer_dn(t, dn_params(j), state_dn[:, j])
        z, _ = trunk_layer(z, c, i, mix, *common)

    return (y, z, new_attn_k, new_attn_v, new_s5_re, new_s5_im, new_na_k, new_na_v, new_dn)
```

```python
import functools
import math

import numpy as np
import jax
import jax.numpy as jnp
from jax import lax
from jax.experimental import pallas as pl
from jax.experimental.pallas import tpu as pltpu

F32 = jnp.float32
BF16 = jnp.bfloat16

D = 1024
CTX_B, CTX_L = 16, 256
LAT_B, LAT_L = 2, 4096
N_CTX = CTX_B * CTX_L
N_LAT = LAT_B * LAT_L
N_TOK = N_CTX + N_LAT
PAST = 512
D_FF = 2816
EPS = 1e-6
NEG = -1e30
GRID_W = 64
HEAD = 64
A_HEADS, A_KV = 16, 4
A_WIN = 128
C_HEADS = 16
NA_ROWS, NA_COLS = 8, 16
ROPE_BASE = 10000.0
S5_GROUPS, S5_GROUP, S5_STATE = 64, 16, 64
S5_N = S5_GROUPS * S5_STATE
DN_QK, DN_V, DN_HD, DN_CONV, DN_CHUNK = 4, 8, 128, 5, 64
DN_NQK = DN_QK * DN_HD
DN_NV = DN_V * DN_HD
DN_PROJ = 2 * DN_NQK + 2 * DN_NV + 128

TM = 512
SEQ_BLK = 256
VMEM_MB = 56
ADA_TN = 1152


def _cparams(sem, mb=VMEM_MB):
    return pltpu.CompilerParams(dimension_semantics=sem, vmem_limit_bytes=mb * 1024 * 1024)


def _resident(block, index_map):
    return pl.BlockSpec(block, index_map, pipeline_mode=pl.Buffered(1))


def _seg_of_tile(i, tm):
    nct = N_CTX // tm
    return jnp.where(i < nct, 0, 1 + (i - nct) // (LAT_L // tm))


def _dot(a, b):
    return jnp.dot(a, b, preferred_element_type=F32)


def _dot_nt(a, b):
    return lax.dot_general(a, b, (((1,), (1,)), ((), ())), preferred_element_type=F32)


def _dot_tn(a, b):
    return lax.dot_general(a, b, (((0,), (0,)), ((), ())), preferred_element_type=F32)


def _dot_f32(a, b):
    return jnp.dot(a, b, preferred_element_type=F32, precision=lax.Precision.HIGHEST)


def _silu(x):
    return x * jax.nn.sigmoid(x)


def _modulate(x, g, m):
    xn = x * lax.rsqrt(jnp.mean(x * x, axis=-1, keepdims=True) + EPS) * g
    return xn * (1.0 + m[:, D:2 * D]) + m[:, :D]


def _residual(x, y, g, m, weight):
    yn = y * lax.rsqrt(jnp.mean(y * y, axis=-1, keepdims=True) + EPS) * g
    return x + weight * m[:, 2 * D:] * yn


def _ada_kernel(cond_ref, w_ref, b_ref, o_ref):
    s = _silu(cond_ref[...]).astype(BF16)
    o_ref[0] = _dot(s, w_ref[0].astype(BF16)) + b_ref[0]


def _ada(cond8, w_ada, b_ada):
    depth, _, n9 = w_ada.shape
    return pl.pallas_call(
        _ada_kernel,
        out_shape=jax.ShapeDtypeStruct((depth, 8, n9), F32),
        grid=(depth, n9 // ADA_TN),
        in_specs=[pl.BlockSpec((8, D), lambda l, j: (0, 0)),
                  pl.BlockSpec((1, D, ADA_TN), lambda l, j: (l, 0, j)),
                  pl.BlockSpec((1, 1, ADA_TN), lambda l, j: (l, 0, j))],
        out_specs=pl.BlockSpec((1, 8, ADA_TN), lambda l, j: (l, 0, j)),
        compiler_params=_cparams(("parallel", "parallel")),
        name="ada",
    )(cond8, w_ada, b_ada.reshape(depth, 1, n9))


FF_CW = 1408


def _ffn_kernel(x_ref, mod_ref, gpre_ref, gpost_ref, wgu_ref, wd_ref, o_ref, act_ref):
    x = x_ref[...]
    m = mod_ref[0]
    h = _modulate(x, gpre_ref[0], m).astype(BF16)
    for c in range(D_FF // FF_CW):
        g = _dot(h, wgu_ref[:, c * FF_CW:(c + 1) * FF_CW])
        u = _dot(h, wgu_ref[:, D_FF + c * FF_CW:D_FF + (c + 1) * FF_CW])
        act_ref[:, c * FF_CW:(c + 1) * FF_CW] = (_silu(g) * u).astype(BF16)
    y = _dot(act_ref[...], wd_ref[...])
    o_ref[...] = _residual(x, y, gpost_ref[0], m, 0.5)


def _ffn(x, mods, ng, wgu, wd, layer, j):
    s = 2 * j
    return pl.pallas_call(
        _ffn_kernel,
        out_shape=jax.ShapeDtypeStruct((N_TOK, D), F32),
        grid=(N_TOK // TM,),
        in_specs=[pl.BlockSpec((TM, D), lambda i: (i, 0)),
                  pl.BlockSpec((1, 1, 3 * D), lambda i: (layer * 3 + _seg_of_tile(i, TM), 0, s)),
                  pl.BlockSpec((1, 1, D), lambda i: (layer * 6 + 2 * s, 0, 0)),
                  pl.BlockSpec((1, 1, D), lambda i: (layer * 6 + 2 * s + 1, 0, 0)),
                  _resident((None, None, D, 2 * D_FF), lambda i: (layer, j, 0, 0)),
                  _resident((None, None, D_FF, D), lambda i: (layer, j, 0, 0))],
        out_specs=pl.BlockSpec((TM, D), lambda i: (i, 0)),
        scratch_shapes=[pltpu.VMEM((TM, D_FF), BF16)],
        compiler_params=_cparams(("parallel",)),
        name="ffn",
    )(x, mods, ng, ng, wgu, wd)


def _proj_kernel(x_ref, mod_ref, g_ref, w_ref, o_ref):
    h = _modulate(x_ref[...], g_ref[0], mod_ref[0]).astype(BF16)
    o_ref[...] = _dot(h, w_ref[...])


def _proj(x, mods, ng, w, layer):
    n = w.shape[1]
    return pl.pallas_call(
        _proj_kernel,
        out_shape=jax.ShapeDtypeStruct((N_TOK, n), F32),
        grid=(N_TOK // TM,),
        in_specs=[pl.BlockSpec((TM, D), lambda i: (i, 0)),
                  pl.BlockSpec((1, 1, 3 * D), lambda i: (layer * 3 + _seg_of_tile(i, TM), 0, 1)),
                  pl.BlockSpec((1, 1, D), lambda i: (layer * 6 + 2, 0, 0)),
                  _resident((D, n), lambda i: (0, 0))],
        out_specs=pl.BlockSpec((TM, n), lambda i: (i, 0)),
        compiler_params=_cparams(("parallel",)),
        name="proj",
    )(x, mods, ng, w)


def _out_pair_kernel(actx_ref, alat_ref, w_ref, x_ref, mod_ref, g_ref, o_ref):
    i = pl.program_id(0)
    a = jnp.where(i < N_CTX // TM, actx_ref[...], alat_ref[...])
    y = _dot(a.astype(BF16), w_ref[...])
    o_ref[...] = _residual(x_ref[...], y, g_ref[0], mod_ref[0], 1.0)


def _out_s5_kernel(yf_ref, yb_ref, w_ref, x_ref, mod_ref, g_ref, o_ref):
    a = jax.nn.gelu(yf_ref[...] + yb_ref[...]).astype(BF16)
    t = _dot(a, w_ref[...])
    y = t[:, :D] * jax.nn.sigmoid(t[:, D:])
    o_ref[...] = _residual(x_ref[...], y, g_ref[0], mod_ref[0], 1.0)


def _out_dn_kernel(of_ref, ob_ref, z_ref, og_ref, w_ref, x_ref, mod_ref, g_ref, o_ref, a_ref):
    o = of_ref[...] + ob_ref[...]
    og = og_ref[...]
    for h in range(DN_V):
        cs = slice(h * DN_HD, (h + 1) * DN_HD)
        oh = o[:, cs]
        on = oh * lax.rsqrt(jnp.mean(oh * oh, axis=-1, keepdims=True) + EPS) * og
        a_ref[:, cs] = (on * _silu(z_ref[:, cs])).astype(BF16)
    y = _dot(a_ref[...], w_ref[...])
    o_ref[...] = _residual(x_ref[...], y, g_ref[0], mod_ref[0], 1.0)


def _tail_specs(layer):
    return [pl.BlockSpec((TM, D), lambda i: (i, 0)),
            pl.BlockSpec((1, 1, 3 * D), lambda i: (layer * 3 + _seg_of_tile(i, TM), 0, 1)),
            pl.BlockSpec((1, 1, D), lambda i: (layer * 6 + 3, 0, 0))]


def _out_pair(actx, alat, w, x, mods, ng, layer):
    nct = N_CTX // TM
    return pl.pallas_call(
        _out_pair_kernel,
        out_shape=jax.ShapeDtypeStruct((N_TOK, D), F32),
        grid=(N_TOK // TM,),
        in_specs=[pl.BlockSpec((TM, D), lambda i: (jnp.minimum(i, nct - 1), 0)),
                  pl.BlockSpec((TM, D), lambda i: (jnp.maximum(i - nct, 0), 0)),
                  _resident((D, D), lambda i: (0, 0))] + _tail_specs(layer),
        out_specs=pl.BlockSpec((TM, D), lambda i: (i, 0)),
        compiler_params=_cparams(("parallel",)),
        name="out_pair",
    )(actx, alat, w, x, mods, ng)


def _out_s5(yf, yb, w, x, mods, ng, layer):
    return pl.pallas_call(
        _out_s5_kernel,
        out_shape=jax.ShapeDtypeStruct((N_TOK, D), F32),
        grid=(N_TOK // TM,),
        in_specs=[pl.BlockSpec((TM, D), lambda i: (i, 0)),
                  pl.BlockSpec((TM, D), lambda i: (i, 0)),
                  _resident((D, 2 * D), lambda i: (0, 0))] + _tail_specs(layer),
        out_specs=pl.BlockSpec((TM, D), lambda i: (i, 0)),
        compiler_params=_cparams(("parallel",)),
        name="out_s5",
    )(yf, yb, w, x, mods, ng)


def _out_dn(of, ob, proj, og, w, x, mods, ng, layer):
    zblk = 2 * DN_NQK + DN_NV
    return pl.pallas_call(
        _out_dn_kernel,
        out_shape=jax.ShapeDtypeStruct((N_TOK, D), F32),
        grid=(N_TOK // TM,),
        in_specs=[pl.BlockSpec((TM, D), lambda i: (i, 0)),
                  pl.BlockSpec((TM, D), lambda i: (i, 0)),
                  pl.BlockSpec((TM, DN_NV), lambda i: (i, zblk // DN_NV)),
                  pl.BlockSpec((1, DN_HD), lambda i: (0, 0)),
                  _resident((DN_NV, D), lambda i: (0, 0))] + _tail_specs(layer),
        out_specs=pl.BlockSpec((TM, D), lambda i: (i, 0)),
        scratch_shapes=[pltpu.VMEM((TM, DN_NV), BF16)],
        compiler_params=_cparams(("parallel",)),
        name="out_dn",
    )(of, ob, proj, og, w, x, mods, ng)


def _softmax_pv(parts, sink):
    m = functools.reduce(jnp.maximum, [jnp.max(s, axis=-1, keepdims=True) for s, _ in parts])
    if sink is not None:
        m = jnp.maximum(m, sink)
    l = None
    o = None
    for s, v in parts:
        p = jnp.exp(s - m)
        ls = jnp.sum(p, axis=-1, keepdims=True)
        os_ = _dot(p.astype(BF16), v)
        l = ls if l is None else l + ls
        o = os_ if o is None else o + os_
    if sink is not None:
        l = l + jnp.exp(sink - m)
    return o / l


def _attn_ctx_kernel(*refs, n_heads, group, has_sink):
    if has_sink:
        sink_ref, q_ref, k_ref, v_ref, o_ref = refs
    else:
        q_ref, k_ref, v_ref, o_ref = refs
    scale = HEAD ** -0.5
    for h in range(n_heads):
        kv = h // group
        q = (q_ref[:, h * HEAD:(h + 1) * HEAD] * scale).astype(BF16)
        k = k_ref[:, kv * HEAD:(kv + 1) * HEAD].astype(BF16)
        v = v_ref[:, kv * HEAD:(kv + 1) * HEAD].astype(BF16)
        s = _dot_nt(q, k)
        o = _softmax_pv([(s, v)], sink_ref[h] if has_sink else None)
        o_ref[:, h * HEAD:(h + 1) * HEAD] = o


def _attn_ctx(qkv, n_heads, n_kv, sink=None):
    qw, kw = n_heads * HEAD, n_kv * HEAD
    kern = functools.partial(_attn_ctx_kernel, n_heads=n_heads, group=n_heads // n_kv,
                             has_sink=sink is not None)
    specs = [pl.BlockSpec((CTX_L, qw), lambda b: (b, 0)),
             pl.BlockSpec((CTX_L, kw), lambda b: (b, qw // kw)),
             pl.BlockSpec((CTX_L, kw), lambda b: (b, qw // kw + 1))]
    args = [qkv, qkv, qkv]
    if sink is not None:
        specs = [pl.BlockSpec(memory_space=pltpu.SMEM)] + specs
        args = [sink] + args
    return pl.pallas_call(
        kern,
        out_shape=jax.ShapeDtypeStruct((N_CTX, qw), F32),
        grid=(CTX_B,),
        in_specs=specs,
        out_specs=pl.BlockSpec((CTX_L, qw), lambda b: (b, 0)),
        compiler_params=_cparams(("parallel",)),
        name="attn_ctx",
    )(*args)


def _rope_tables():
    n = HEAD // 4
    inv = ROPE_BASE ** (-jnp.arange(n, dtype=F32) / n)
    t = jnp.arange(LAT_L)
    ang_r = (t // GRID_W).astype(F32)[:, None] * inv[None, :]
    ang_c = (t % GRID_W).astype(F32)[:, None] * inv[None, :]
    cos = jnp.concatenate([jnp.cos(ang_r), jnp.cos(ang_r), jnp.cos(ang_c), jnp.cos(ang_c)], axis=-1)
    sin = jnp.concatenate([-jnp.sin(ang_r), jnp.sin(ang_r), -jnp.sin(ang_c), jnp.sin(ang_c)], axis=-1)
    return jnp.tile(cos, (1, 2)), jnp.tile(sin, (1, 2))


def _rope(x, cos, sin):
    rows = x.shape[0]
    lane = lax.broadcasted_iota(jnp.int32, (rows, 128), 1)
    first = (lane % 32) < 16
    outs = []
    for cb in range(x.shape[1] // 128):
        xb = x[:, cb * 128:(cb + 1) * 128]
        partner = jnp.where(first, pltpu.roll(xb, 112, 1), pltpu.roll(xb, 16, 1))
        outs.append(xb * cos + partner * sin)
    return jnp.concatenate(outs, axis=1)


A_KWIN = 3 * A_WIN


def _attn_a_lat_kernel(sink_ref, q_ref, k_ref, v_ref, kc_ref, vc_ref, cos_ref, sin_ref, o_ref):
    n = pl.program_id(1)
    nb = LAT_L // A_WIN
    start = pl.multiple_of(jnp.clip(n - 1, 0, nb - 3) * A_WIN, A_WIN)
    q0 = pl.multiple_of(n * A_WIN, A_WIN)
    scale = HEAD ** -0.5
    q = _rope(q_ref[...], cos_ref[pl.ds(q0, A_WIN), :], sin_ref[pl.ds(q0, A_WIN), :]) * scale
    k = _rope(k_ref[pl.ds(start, A_KWIN), :], cos_ref[pl.ds(start, A_KWIN), :],
              sin_ref[pl.ds(start, A_KWIN), :]).astype(BF16)
    v = v_ref[pl.ds(start, A_KWIN), :].astype(BF16)
    kc = kc_ref[0].astype(BF16)
    vc = vc_ref[0].astype(BF16)
    qpos = q0 + lax.broadcasted_iota(jnp.int32, (A_WIN, A_KWIN), 0)
    kpos = start + lax.broadcasted_iota(jnp.int32, (A_WIN, A_KWIN), 1)
    ok = jnp.abs(kpos - qpos) <= A_WIN
    group = A_HEADS // A_KV
    for h in range(A_HEADS):
        kvs = slice((h // group) * HEAD, (h // group + 1) * HEAD)
        qh = q[:, h * HEAD:(h + 1) * HEAD].astype(BF16)
        s_loc = jnp.where(ok, _dot_nt(qh, k[:, kvs]), NEG)
        s_ctx = _dot_nt(qh, kc[:, kvs])
        o = _softmax_pv([(s_loc, v[:, kvs]), (s_ctx, vc[:, kvs])], sink_ref[h])
        o_ref[:, h * HEAD:(h + 1) * HEAD] = o


def _attn_a_lat(qkv, sink, kc, vc, cos, sin):
    qw, kw = A_HEADS * HEAD, A_KV * HEAD
    nb = LAT_L // A_WIN
    return pl.pallas_call(
        _attn_a_lat_kernel,
        out_shape=jax.ShapeDtypeStruct((N_LAT, qw), F32),
        grid=(LAT_B, nb),
        in_specs=[pl.BlockSpec(memory_space=pltpu.SMEM),
                  pl.BlockSpec((A_WIN, qw), lambda b, n: (N_CTX // A_WIN + b * nb + n, 0)),
                  pl.BlockSpec((LAT_L, kw), lambda b, n: (N_CTX // LAT_L + b, qw // kw)),
                  pl.BlockSpec((LAT_L, kw), lambda b, n: (N_CTX // LAT_L + b, qw // kw + 1)),
                  pl.BlockSpec((1, PAST, kw), lambda b, n: (b, 0, 0)),
                  pl.BlockSpec((1, PAST, kw), lambda b, n: (b, 0, 0)),
                  _resident((LAT_L, 128), lambda b, n: (0, 0)),
                  _resident((LAT_L, 128), lambda b, n: (0, 0))],
        out_specs=pl.BlockSpec((A_WIN, qw), lambda b, n: (b * nb + n, 0)),
        compiler_params=_cparams(("parallel", "parallel")),
        name="attn_a_lat",
    )(sink, qkv, qkv, qkv, kc, vc, cos, sin)


NA_KEYS = NA_ROWS * GRID_W


def _na_bias_table(rpb):
    col = jnp.arange(GRID_W)
    dcol = jnp.clip(col[None, :] - col[:, None], 1 - NA_COLS, NA_COLS - 1) + NA_COLS - 1
    cs = jnp.clip(col - NA_COLS // 2, 0, GRID_W - NA_COLS)
    ok = (col[None, :] >= cs[:, None]) & (col[None, :] < cs[:, None] + NA_COLS)
    t = jnp.where(ok[None, None], rpb[:, :, dcol], NEG)
    idx = jnp.arange(NA_ROWS)[:, None] + jnp.arange(NA_ROWS)[None, :]
    bt = t[:, idx]
    return jnp.transpose(bt, (1, 0, 3, 2, 4)).reshape(NA_ROWS, C_HEADS, GRID_W, NA_KEYS)


def _attn_c_lat_kernel(q_ref, k_ref, v_ref, kc_ref, vc_ref, bias_ref, o_ref):
    r = pl.program_id(1)
    rows = LAT_L // GRID_W
    start = pl.multiple_of(jnp.clip(r - NA_ROWS // 2, 0, rows - NA_ROWS) * GRID_W, GRID_W)
    scale = HEAD ** -0.5
    q = q_ref[...] * scale
    for h in range(C_HEADS):
        hs = slice(h * HEAD, (h + 1) * HEAD)
        qh = q[:, hs].astype(BF16)
        s_loc = _dot_nt(qh, k_ref[pl.ds(start, NA_KEYS), hs]) + bias_ref[h]
        s_ctx = _dot_nt(qh, kc_ref[0, :, hs].astype(BF16))
        o = _softmax_pv([(s_loc, v_ref[pl.ds(start, NA_KEYS), hs]),
                         (s_ctx, vc_ref[0, :, hs].astype(BF16))], None)
        o_ref[:, hs] = o


def _attn_c_lat(qkv, kv_bf, kc, vc, bias):
    qw = C_HEADS * HEAD
    rows = LAT_L // GRID_W

    def bias_idx(b, r):
        return (jnp.clip(r - NA_ROWS // 2, 0, rows - NA_ROWS) - r + NA_ROWS - 1, 0, 0, 0)

    return pl.pallas_call(
        _attn_c_lat_kernel,
        out_shape=jax.ShapeDtypeStruct((N_LAT, qw), F32),
        grid=(LAT_B, rows),
        in_specs=[pl.BlockSpec((GRID_W, qw), lambda b, r: (N_CTX // GRID_W + b * rows + r, 0)),
                  _resident((LAT_L, qw), lambda b, r: (b, 0)),
                  _resident((LAT_L, qw), lambda b, r: (b, 1)),
                  pl.BlockSpec((1, PAST, qw), lambda b, r: (b, 0, 0)),
                  pl.BlockSpec((1, PAST, qw), lambda b, r: (b, 0, 0)),
                  pl.BlockSpec((None, C_HEADS, GRID_W, NA_KEYS), bias_idx)],
        out_specs=pl.BlockSpec((GRID_W, qw), lambda b, r: (b * rows + r, 0)),
        compiler_params=_cparams(("parallel", "arbitrary")),
        name="attn_c_lat",
    )(qkv, kv_bf, kv_bf, kc, vc, bias)


def _seq_tables():
    fb, bb, first, last, sid = [], [], [], [], []
    base = 0
    for s, length in enumerate([CTX_L] * CTX_B + [LAT_L] * LAT_B):
        n = length // SEQ_BLK
        for c in range(n):
            fb.append(base + c)
            bb.append(base + n - 1 - c)
            first.append(int(c == 0))
            last.append(int(c == n - 1))
            sid.append(s)
        base += n
    return tuple(jnp.asarray(np.array(t, np.int32)) for t in (fb, bb, first, last, sid))


def _seg_of_sid(s):
    return jnp.maximum(s - (CTX_B - 1), 0)


S5_GB = 8
S5_LB = 1024
S5_TILES = SEQ_BLK // 8


def _s5_tables(lam_re, lam_im, log_dt, b_re, b_im, c_re, c_im):
    dt = jnp.exp(log_dt)[..., None]
    lr, li = lam_re * dt, lam_im * dt
    a_re, a_im = jnp.exp(lr) * jnp.cos(li), jnp.exp(lr) * jnp.sin(li)
    den = lam_re * lam_re + lam_im * lam_im
    fr = ((a_re - 1.0) * lam_re + a_im * lam_im) / den
    fi = (a_im * lam_re - (a_re - 1.0) * lam_im) / den
    bb_re = fr[..., None] * b_re - fi[..., None] * b_im
    bb_im = fr[..., None] * b_im + fi[..., None] * b_re
    eye = jnp.eye(S5_GB, dtype=F32)

    def bdiag_in(t):
        t = t.reshape(2, S5_GROUPS // S5_GB, S5_GB, S5_STATE, S5_GROUP)
        return jnp.einsum('dbgpc,gh->dbgchp', t, eye).reshape(2, S5_GROUPS // S5_GB, 128, 512)

    def bdiag_out(t):
        t = t.reshape(2, S5_GROUPS // S5_GB, S5_GB, S5_GROUP, S5_STATE)
        return jnp.einsum('dbgcp,gh->dbgphc', t, eye).reshape(2, S5_GROUPS // S5_GB, 512, 128)

    wb = jnp.stack([bdiag_in(bb_re), bdiag_in(bb_im)], axis=1).astype(BF16)
    wc = jnp.stack([bdiag_out(c_re), bdiag_out(c_im)], axis=1).astype(BF16)

    ar, ai = a_re.reshape(2, S5_N), a_im.reshape(2, S5_N)
    pows = [(ar, ai)]
    for _ in range(7):
        pr, pi = pows[-1]
        pows.append((pr * ar - pi * ai, pr * ai + pi * ar))
    row = jnp.arange(8)[:, None]
    zero = jnp.zeros((8, S5_N), F32)
    ta, pt = [], []
    for d in range(2):
        steps = []
        for s in (1, 2, 4):
            keep = (row >= s) if d == 0 else (row + s <= 7)
            steps.append(jnp.stack([jnp.where(keep, pows[s - 1][ri][d][None, :], zero) for ri in range(2)]))
        ta.append(jnp.stack(steps))
        order = list(range(8)) if d == 0 else list(range(7, -1, -1))
        pt.append(jnp.stack([jnp.stack([pows[k][ri][d] for k in order]) for ri in range(2)]))
    return wb, wc, jnp.stack(ta), jnp.stack(pt)


def _s5_kernel(fb_ref, bb_ref, first_ref, last_ref, sid_ref,
               xf_ref, xb_ref, mod_ref, g_ref, dsk_ref, wb_ref, wc_ref, ta_ref, pt_ref, sre_ref, sim_ref,
               yf_ref, yb_ref, ore_ref, oim_ref, bur, bui, car):
    i = pl.program_id(0)
    is_first = first_ref[i] == 1
    is_lat = sid_ref[i] >= CTX_B

    @pl.when(is_first & is_lat)
    def _():
        car[:, 0] = sre_ref[0]
        car[:, 1] = sim_ref[0]

    @pl.when(is_first & jnp.logical_not(is_lat))
    def _():
        car[...] = jnp.zeros_like(car)

    for d in range(2):
        x_ref, y_ref = (xf_ref, yf_ref) if d == 0 else (xb_ref, yb_ref)
        hm = _modulate(x_ref[...], g_ref[0], mod_ref[0])
        hb = hm.astype(BF16)
        for gb in range(S5_GROUPS // S5_GB):
            ub = hb[:, gb * 128:(gb + 1) * 128]
            bur[:, gb * 512:(gb + 1) * 512] = _dot(ub, wb_ref[d, 0, gb])
            bui[:, gb * 512:(gb + 1) * 512] = _dot(ub, wb_ref[d, 1, gb])

        last_row = 7 if d == 0 else 0
        for lb in range(S5_N // S5_LB):
            cs = slice(lb * S5_LB, (lb + 1) * S5_LB)

            def body(t, carry, d=d, cs=cs, last_row=last_row):
                cr, ci = carry
                tt = t if d == 0 else S5_TILES - 1 - t
                r0 = pl.multiple_of(tt * 8, 8)
                xr = bur[pl.ds(r0, 8), cs]
                xi = bui[pl.ds(r0, 8), cs]
                for si, s in enumerate((1, 2, 4)):
                    sh = s if d == 0 else 8 - s
                    sr = pltpu.roll(xr, sh, 0)
                    sm = pltpu.roll(xi, sh, 0)
                    ar = ta_ref[d, si, 0, :, cs]
                    ai = ta_ref[d, si, 1, :, cs]
                    xr, xi = xr + ar * sr - ai * sm, xi + ar * sm + ai * sr
                pr = pt_ref[d, 0, :, cs]
                pi = pt_ref[d, 1, :, cs]
                xr, xi = xr + pr * cr - pi * ci, xi + pr * ci + pi * cr
                bur[pl.ds(r0, 8), cs] = xr
                bui[pl.ds(r0, 8), cs] = xi
                return xr[last_row:last_row + 1], xi[last_row:last_row + 1]

            cr, ci = lax.fori_loop(0, S5_TILES, body, (car[d, 0, :, cs], car[d, 1, :, cs]))
            car[d, 0, :, cs] = cr
            car[d, 1, :, cs] = ci

        for gb in range(S5_GROUPS // S5_GB):
            xr = bur[:, gb * 512:(gb + 1) * 512].astype(BF16)
            xi = bui[:, gb * 512:(gb + 1) * 512].astype(BF16)
            y = _dot(xr, wc_ref[d, 0, gb]) - _dot(xi, wc_ref[d, 1, gb])
            if d == 0:
                y = y + dsk_ref[:, gb * 128:(gb + 1) * 128] * hm[:, gb * 128:(gb + 1) * 128]
            y_ref[:, gb * 128:(gb + 1) * 128] = y

    ore_ref[0] = car[:, 0]
    oim_ref[0] = car[:, 1]


def _s5(x, mods, ng, dskip, tabs, state_re, state_im, layer, tables):
    wb, wc, ta, pt = tabs
    n_steps = tables[0].shape[0]
    n_seq = CTX_B + LAT_B

    def mod_idx(i, fb, bb, first, last, sid):
        return (layer * 3 + _seg_of_sid(sid[i]), 0, 1)

    def st_idx(i, fb, bb, first, last, sid):
        return (jnp.maximum(sid[i] - CTX_B, 0), 0, 0, 0)

    def const(nd):
        return lambda i, *_: (0,) * nd

    grid_spec = pltpu.PrefetchScalarGridSpec(
        num_scalar_prefetch=5,
        grid=(n_steps,),
        in_specs=[pl.BlockSpec((SEQ_BLK, D), lambda i, fb, *_: (fb[i], 0)),
                  pl.BlockSpec((SEQ_BLK, D), lambda i, fb, bb, *_: (bb[i], 0)),
                  pl.BlockSpec((1, 1, 3 * D), mod_idx),
                  pl.BlockSpec((1, 1, D), lambda i, *_: (layer * 6 + 2, 0, 0)),
                  pl.BlockSpec((1, D), const(2)),
                  _resident(wb.shape, const(5)),
                  _resident(wc.shape, const(5)),
                  _resident(ta.shape, const(5)),
                  _resident(pt.shape, const(4)),
                  pl.BlockSpec((1, 2, 1, S5_N), st_idx),
                  pl.BlockSpec((1, 2, 1, S5_N), st_idx)],
        out_specs=[pl.BlockSpec((SEQ_BLK, D), lambda i, fb, *_: (fb[i], 0)),
                   pl.BlockSpec((SEQ_BLK, D), lambda i, fb, bb, *_: (bb[i], 0)),
                   pl.BlockSpec((1, 2, 1, S5_N), lambda i, fb, bb, first, last, sid: (sid[i], 0, 0, 0)),
                   pl.BlockSpec((1, 2, 1, S5_N), lambda i, fb, bb, first, last, sid: (sid[i], 0, 0, 0))],
        scratch_shapes=[pltpu.VMEM((SEQ_BLK, S5_N), F32), pltpu.VMEM((SEQ_BLK, S5_N), F32),
                        pltpu.VMEM((2, 2, 1, S5_N), F32)])
    return pl.pallas_call(
        _s5_kernel,
        out_shape=[jax.ShapeDtypeStruct((N_TOK, D), F32), jax.ShapeDtypeStruct((N_TOK, D), F32),
                   jax.ShapeDtypeStruct((n_seq, 2, 1, S5_N), F32), jax.ShapeDtypeStruct((n_seq, 2, 1, S5_N), F32)],
        grid_spec=grid_spec,
        compiler_params=_cparams(("arbitrary",)),
        name="s5",
    )(*tables, x, x, mods, ng, dskip, wb, wc, ta, pt, state_re, state_im)


def _dn_prep_kernel(blk_ref, first_ref, last_ref, x_ref, prev_ref, next_ref, ba_ref, cw_ref, alog_ref, dtb_ref,
                    q_ref, k_ref, v_ref, g_ref):
    i = pl.program_id(0)
    nqkv = 2 * DN_NQK + DN_NV
    x = x_ref[...]
    pv = jnp.where(first_ref[i] == 1, 0.0, prev_ref[...])
    nx = jnp.where(last_ref[i] == 1, 0.0, next_ref[...])
    ext = jnp.concatenate([pv, x, nx], axis=0)
    n_ext = SEQ_BLK + 16
    acc = None
    for j in range(DN_CONV):
        off = DN_CONV // 2 - j
        e = ext if off == 0 else pltpu.roll(ext, off % n_ext, 0)
        term = cw_ref[j:j + 1, :] * e[8:8 + SEQ_BLK]
        acc = term if acc is None else acc + term
    a = _silu(acc)
    for h in range(DN_QK):
        cs = slice(h * DN_HD, (h + 1) * DN_HD)
        qh = a[:, cs]
        q_ref[:, cs] = qh * lax.rsqrt(jnp.sum(qh * qh, axis=-1, keepdims=True) + EPS) * (DN_HD ** -0.5)
        kh = a[:, DN_NQK + h * DN_HD:DN_NQK + (h + 1) * DN_HD]
        k_ref[:, cs] = kh * lax.rsqrt(jnp.sum(kh * kh, axis=-1, keepdims=True) + EPS)
    v_ref[...] = a[:, 2 * DN_NQK:nqkv]

    ba = ba_ref[...]
    lane = lax.broadcasted_iota(jnp.int32, ba.shape, 1)
    row = lax.broadcasted_iota(jnp.int32, ba.shape, 0) % DN_CHUNK
    is_g = ((lane % 16) >= 8) & (lane < 32)
    z = ba + dtb_ref[...]
    softplus = jnp.maximum(z, 0.0) + jnp.log1p(jnp.exp(-jnp.abs(z)))
    val = jnp.where(is_g, -jnp.exp(alog_ref[...]) * softplus, jax.nn.sigmoid(ba))
    cf = val
    cr = val
    s = 1
    while s < DN_CHUNK:
        cf = cf + jnp.where(row >= s, pltpu.roll(cf, s, 0), 0.0)
        cr = cr + jnp.where(row < DN_CHUNK - s, pltpu.roll(cr, SEQ_BLK - s, 0), 0.0)
        s *= 2
    g_ref[...] = jnp.where(lane < 32, jnp.where(is_g, jnp.where(lane < 16, cf, cr), val), 0.0)


def _dn_prep(proj, conv_w, alog_row, dtb_row, tables):
    fb, _, first, last, _ = tables
    nqkv = 2 * DN_NQK + DN_NV
    per = SEQ_BLK // 8
    n8 = N_TOK // 8
    grid_spec = pltpu.PrefetchScalarGridSpec(
        num_scalar_prefetch=3,
        grid=(fb.shape[0],),
        in_specs=[pl.BlockSpec((SEQ_BLK, nqkv), lambda i, blk, *_: (blk[i], 0)),
                  pl.BlockSpec((8, nqkv), lambda i, blk, *_: (jnp.maximum(blk[i] * per - 1, 0), 0)),
                  pl.BlockSpec((8, nqkv), lambda i, blk, *_: (jnp.minimum(blk[i] * per + per, n8 - 1), 0)),
                  pl.BlockSpec((SEQ_BLK, 128), lambda i, blk, *_: (blk[i], (nqkv + DN_NV) // 128)),
                  pl.BlockSpec((DN_CONV, nqkv), lambda i, *_: (0, 0)),
                  pl.BlockSpec((1, 128), lambda i, *_: (0, 0)),
                  pl.BlockSpec((1, 128), lambda i, *_: (0, 0))],
        out_specs=[pl.BlockSpec((SEQ_BLK, DN_NQK), lambda i, blk, *_: (blk[i], 0)),
                   pl.BlockSpec((SEQ_BLK, DN_NQK), lambda i, blk, *_: (blk[i], 0)),
                   pl.BlockSpec((SEQ_BLK, DN_NV), lambda i, blk, *_: (blk[i], 0)),
                   pl.BlockSpec((SEQ_BLK, 128), lambda i, blk, *_: (blk[i], 0))])
    return pl.pallas_call(
        _dn_prep_kernel,
        out_shape=[jax.ShapeDtypeStruct((N_TOK, DN_NQK), F32), jax.ShapeDtypeStruct((N_TOK, DN_NQK), F32),
                   jax.ShapeDtypeStruct((N_TOK, DN_NV), F32), jax.ShapeDtypeStruct((N_TOK, 128), F32)],
        grid_spec=grid_spec,
        compiler_params=_cparams(("arbitrary",)),
        name="dn_prep",
    )(fb, first, last, proj, proj, proj, proj, conv_w, alog_row, dtb_row)


DN_NCH = SEQ_BLK // DN_CHUNK


def _dn_chunk(d, h, q, k, v, gbc, s_prev):
    c = DN_CHUNK
    lane = lax.broadcasted_iota(jnp.int32, gbc.shape, 1)
    beta = jnp.sum(jnp.where(lane == 16 * d + h, gbc, 0.0), axis=1, keepdims=True)
    gcol = jnp.sum(jnp.where(lane == 16 * d + 8 + h, gbc, 0.0), axis=1, keepdims=True)
    ri = lax.broadcasted_iota(jnp.int32, (c, c), 0)
    ci = lax.broadcasted_iota(jnp.int32, (c, c), 1)
    eye = ri == ci
    grow = jnp.sum(jnp.where(eye, gcol, 0.0), axis=0, keepdims=True)
    lower = (ri >= ci) if d == 0 else (ri <= ci)
    strict = (ri > ci) if d == 0 else (ri < ci)
    decay = jnp.exp(jnp.where(lower, gcol - grow, NEG))
    kb16 = k.astype(BF16)
    kk = _dot_nt(kb16, kb16)
    qk = _dot_nt(q.astype(BF16), kb16)
    a = jnp.where(strict, -(beta * kk * decay), 0.0)
    t = jnp.where(eye, 1.0, 0.0) + a
    p = a
    for _ in range(int(math.log2(c)) - 1):
        p = _dot_f32(p, p)
        t = t + _dot_f32(t, p)
    rhs = jnp.concatenate([v * beta, k * (beta * jnp.exp(gcol))], axis=1)
    sol = _dot_f32(t, rhs)
    u, w = sol[:, :DN_HD], sol[:, DN_HD:]
    aqk = jnp.where(lower, qk * decay, 0.0)
    s16 = s_prev.astype(BF16)
    v_new = u - _dot(w.astype(BF16), s16)
    vn16 = v_new.astype(BF16)
    o = _dot((q * jnp.exp(gcol)).astype(BF16), s16) + _dot(aqk.astype(BF16), vn16)
    last = c - 1 if d == 0 else 0
    g_last = gcol[last:last + 1, :]
    s_new = s_prev * jnp.exp(g_last) + _dot_tn((k * jnp.exp(g_last - gcol)).astype(BF16), vn16)
    return o, s_new


def _dn_kernel(fb_ref, bb_ref, first_ref, last_ref, sid_ref,
               qf_ref, kf_ref, vf_ref, gf_ref, qb_ref, kb_ref, vb_ref, gb_ref, s0_ref,
               of_ref, ob_ref, so_ref, s_ref):
    h = pl.program_id(0)
    i = pl.program_id(1)
    is_first = first_ref[i] == 1
    is_lat = sid_ref[i] >= CTX_B

    @pl.when(is_first & is_lat)
    def _():
        s_ref[...] = s0_ref[0, :, 0]

    @pl.when(is_first & jnp.logical_not(is_lat))
    def _():
        s_ref[...] = jnp.zeros_like(s_ref)

    for d in range(2):
        q_ref, k_ref, v_ref, g_ref, o_ref = ((qf_ref, kf_ref, vf_ref, gf_ref, of_ref) if d == 0
                                             else (qb_ref, kb_ref, vb_ref, gb_ref, ob_ref))
        s_cur = s_ref[d]
        order = range(DN_NCH) if d == 0 else range(DN_NCH - 1, -1, -1)
        for c in order:
            rows = slice(c * DN_CHUNK, (c + 1) * DN_CHUNK)
            o, s_cur = _dn_chunk(d, h, q_ref[rows, :], k_ref[rows, :], v_ref[rows, :], g_ref[rows, :], s_cur)
            o_ref[rows, :] = o
        s_ref[d] = s_cur
    so_ref[0, :, 0] = s_ref[...]


def _dn(qn, kn, vv, gates, state, tables):
    n_steps = tables[0].shape[0]
    n_seq = CTX_B + LAT_B
    rep = DN_V // DN_QK

    def fwd(width_fn):
        return lambda h, i, fb, *_: (fb[i], width_fn(h))

    def bwd(width_fn):
        return lambda h, i, fb, bb, *_: (bb[i], width_fn(h))

    qk_col = lambda h: h // rep
    v_col = lambda h: h
    g_col = lambda h: 0

    def st_in(h, i, fb, bb, first, last, sid):
        return (jnp.maximum(sid[i] - CTX_B, 0), 0, h, 0, 0)

    def st_out(h, i, fb, bb, first, last, sid):
        return (sid[i], 0, h, 0, 0)

    blk = lambda w: (SEQ_BLK, w)
    grid_spec = pltpu.PrefetchScalarGridSpec(
        num_scalar_prefetch=5,
        grid=(DN_V, n_steps),
        in_specs=[pl.BlockSpec(blk(DN_HD), fwd(qk_col)), pl.BlockSpec(blk(DN_HD), fwd(qk_col)),
                  pl.BlockSpec(blk(DN_HD), fwd(v_col)), pl.BlockSpec(blk(128), fwd(g_col)),
                  pl.BlockSpec(blk(DN_HD), bwd(qk_col)), pl.BlockSpec(blk(DN_HD), bwd(qk_col)),
                  pl.BlockSpec(blk(DN_HD), bwd(v_col)), pl.BlockSpec(blk(128), bwd(g_col)),
                  pl.BlockSpec((1, 2, 1, DN_HD, DN_HD), st_in)],
        out_specs=[pl.BlockSpec(blk(DN_HD), fwd(v_col)), pl.BlockSpec(blk(DN_HD), bwd(v_col)),
                   pl.BlockSpec((1, 2, 1, DN_HD, DN_HD), st_out)],
        scratch_shapes=[pltpu.VMEM((2, DN_HD, DN_HD), F32)])
    return pl.pallas_call(
        _dn_kernel,
        out_shape=[jax.ShapeDtypeStruct((N_TOK, DN_NV), F32), jax.ShapeDtypeStruct((N_TOK, DN_NV), F32),
                   jax.ShapeDtypeStruct((n_seq, 2, DN_V, DN_HD, DN_HD), F32)],
        grid_spec=grid_spec,
        compiler_params=_cparams(("arbitrary", "arbitrary")),
        name="dn",
    )(*tables, qn, kn, vv, gates, qn, kn, vv, gates, state)


def kernel(x_prompt, x_sample, cache_attn_k, cache_attn_v, state_s5_re, state_s5_im, cache_na_k, cache_na_v,
           state_dn, c, c_ctx, norm_g, w_ada, b_ada, ffn_w_gu, ffn_w_d, a_w_qkv, a_w_o, a_sink,
           s5_lam_re, s5_lam_im, s5_log_dt, s5_b_re, s5_b_im, s5_c_re, s5_c_im, s5_d, s5_w_glu,
           na_w_qkv, na_w_o, na_rpb, dn_w_in, dn_conv_w, dn_w_ba, dn_a_log, dn_dt_bias, dn_out_g, dn_w_o):
    depth = w_ada.shape[0]
    x = jnp.concatenate([x_prompt.reshape(N_CTX, D), x_sample.reshape(N_LAT, D)], axis=0)
    cond8 = jnp.concatenate([c_ctx[None, :], c, jnp.zeros((8 - 1 - LAT_B, D), F32)], axis=0)
    mods = _ada(cond8, w_ada, b_ada)[:, :1 + LAT_B].reshape(depth * (1 + LAT_B), 1, 9 * D)
    ng = norm_g.reshape(depth * 6, 1, D)
    wgu = ffn_w_gu.astype(BF16)
    wd = ffn_w_d.astype(BF16)
    tables = _seq_tables()

    x = _ffn(x, mods, ng, wgu, wd, 0, 0)
    qkv = _proj(x, mods, ng, a_w_qkv[0].astype(BF16), 0)
    qw, kw = A_HEADS * HEAD, A_KV * HEAD
    cos, sin = _rope_tables()
    o_ctx = _attn_ctx(qkv, A_HEADS, A_KV, a_sink[0])
    o_lat = _attn_a_lat(qkv, a_sink[0], cache_attn_k[:, 0].reshape(LAT_B, PAST, kw),
                        cache_attn_v[:, 0].reshape(LAT_B, PAST, kw), cos, sin)
    new_attn_k = qkv[:N_CTX, qw:qw + kw].reshape(CTX_B, 1, CTX_L, A_KV, HEAD)
    new_attn_v = qkv[:N_CTX, qw + kw:].reshape(CTX_B, 1, CTX_L, A_KV, HEAD)
    x = _out_pair(o_ctx, o_lat, a_w_o[0].astype(BF16), x, mods, ng, 0)
    x = _ffn(x, mods, ng, wgu, wd, 0, 1)

    x = _ffn(x, mods, ng, wgu, wd, 1, 0)
    tabs = _s5_tables(s5_lam_re[0], s5_lam_im[0], s5_log_dt[0], s5_b_re[0], s5_b_im[0], s5_c_re[0], s5_c_im[0])
    yf, yb, fin_re, fin_im = _s5(x, mods, ng, s5_d, tabs,
                                 state_s5_re[:, 0].reshape(LAT_B, 2, 1, S5_N),
                                 state_s5_im[:, 0].reshape(LAT_B, 2, 1, S5_N), 1, tables)
    new_s5_re = fin_re[:CTX_B].reshape(CTX_B, 1, 2, S5_GROUPS, S5_STATE)
    new_s5_im = fin_im[:CTX_B].reshape(CTX_B, 1, 2, S5_GROUPS, S5_STATE)
    x = _out_s5(yf, yb, s5_w_glu[0].astype(BF16), x, mods, ng, 1)
    x = _ffn(x, mods, ng, wgu, wd, 1, 1)

    x = _ffn(x, mods, ng, wgu, wd, 2, 0)
    qkv = _proj(x, mods, ng, na_w_qkv[0].astype(BF16), 2)
    cw = C_HEADS * HEAD
    o_ctx = _attn_ctx(qkv, C_HEADS, C_HEADS)
    o_lat = _attn_c_lat(qkv, qkv[N_CTX:, cw:].astype(BF16), cache_na_k[:, 0].reshape(LAT_B, PAST, cw),
                        cache_na_v[:, 0].reshape(LAT_B, PAST, cw), _na_bias_table(na_rpb[0]))
    new_na_k = qkv[:N_CTX, cw:2 * cw].reshape(CTX_B, 1, CTX_L, C_HEADS, HEAD)
    new_na_v = qkv[:N_CTX, 2 * cw:].reshape(CTX_B, 1, CTX_L, C_HEADS, HEAD)
    x = _out_pair(o_ctx, o_lat, na_w_o[0].astype(BF16), x, mods, ng, 2)
    x = _ffn(x, mods, ng, wgu, wd, 2, 1)

    x = _ffn(x, mods, ng, wgu, wd, 3, 0)
    w_all = jnp.concatenate([dn_w_in[0], dn_w_ba[0, 0], dn_w_ba[0, 1],
                             jnp.zeros((D, 128 - 4 * DN_V), F32)], axis=1).astype(BF16)
    proj = _proj(x, mods, ng, w_all, 3)
    pad8 = jnp.zeros((DN_V,), F32)
    gate_row = lambda t: jnp.concatenate([pad8, t[0], pad8, t[1], jnp.zeros((128 - 4 * DN_V,), F32)])[None, :]
    qn, kn, vv, gates = _dn_prep(proj, dn_conv_w[0], gate_row(dn_a_log[0]), gate_row(dn_dt_bias[0]), tables)
    of, ob, fin_dn = _dn(qn, kn, vv, gates, state_dn[:, 0], tables)
    new_dn = fin_dn[:CTX_B][:, None]
    x = _out_dn(of, ob, proj, dn_out_g, dn_w_o[0].astype(BF16), x, mods, ng, 3)
    x = _ffn(x, mods, ng, wgu, wd, 3, 1)

    y = x[:N_CTX].reshape(CTX_B, CTX_L, D)
    z = x[N_CTX:].reshape(LAT_B, LAT_L, D)
    return (y, z, new_attn_k, new_attn_v, new_s5_re, new_s5_im, new_na_k, new_na_v, new_dn)
```

```python
import functools
import math

import numpy as np
import jax
import jax.numpy as jnp
from jax import lax
from jax.experimental import pallas as pl
from jax.experimental.pallas import tpu as pltpu

F32 = jnp.float32
BF16 = jnp.bfloat16

D = 1024
CTX_B, CTX_L = 16, 256
LAT_B, LAT_L = 2, 4096
N_CTX = CTX_B * CTX_L
N_LAT = LAT_B * LAT_L
N_TOK = N_CTX + N_LAT
PAST = 512
D_FF = 2816
EPS = 1e-6
NEG = -1e30
GRID_W = 64
HEAD = 64
A_HEADS, A_KV = 16, 4
A_WIN = 128
C_HEADS = 16
NA_ROWS, NA_COLS = 8, 16
ROPE_BASE = 10000.0
S5_GROUPS, S5_GROUP, S5_STATE = 64, 16, 64
S5_N = S5_GROUPS * S5_STATE
DN_QK, DN_V, DN_HD, DN_CONV, DN_CHUNK = 4, 8, 128, 5, 64
DN_REP = DN_V // DN_QK
DN_NQK = DN_QK * DN_HD
DN_NV = DN_V * DN_HD
DN_PROJ = 2 * DN_NQK + 2 * DN_NV + 128

TM = 512
SEQ_BLK = 256
VMEM_MB = 56
ADA_TN = 1152


def _cparams(sem, mb=VMEM_MB):
    return pltpu.CompilerParams(dimension_semantics=sem, vmem_limit_bytes=mb * 1024 * 1024)


def _resident(block, index_map):
    return pl.BlockSpec(block, index_map, pipeline_mode=pl.Buffered(1))


def _seg_of_tile(i, tm):
    nct = N_CTX // tm
    return jnp.where(i < nct, 0, 1 + (i - nct) // (LAT_L // tm))


def _dot(a, b):
    return jnp.dot(a, b, preferred_element_type=F32)


def _dot_nt(a, b):
    return lax.dot_general(a, b, (((1,), (1,)), ((), ())), preferred_element_type=F32)


def _dot_tn(a, b):
    return lax.dot_general(a, b, (((0,), (0,)), ((), ())), preferred_element_type=F32)


def _silu(x):
    return x * jax.nn.sigmoid(x)


def _modulate(x, g, m):
    xn = x * lax.rsqrt(jnp.mean(x * x, axis=-1, keepdims=True) + EPS) * g
    return xn * (1.0 + m[:, D:2 * D]) + m[:, :D]


def _residual(x, y, g, m, weight):
    yn = y * lax.rsqrt(jnp.mean(y * y, axis=-1, keepdims=True) + EPS) * g
    return x + weight * m[:, 2 * D:] * yn


def _ada_kernel(cond_ref, w_ref, b_ref, o_ref):
    s = _silu(cond_ref[...]).astype(BF16)
    o_ref[0] = _dot(s, w_ref[0].astype(BF16)) + b_ref[0]


def _ada(cond8, w_ada, b_ada):
    depth, _, n9 = w_ada.shape
    return pl.pallas_call(
        _ada_kernel,
        out_shape=jax.ShapeDtypeStruct((depth, 8, n9), F32),
        grid=(depth, n9 // ADA_TN),
        in_specs=[pl.BlockSpec((8, D), lambda l, j: (0, 0)),
                  pl.BlockSpec((1, D, ADA_TN), lambda l, j: (l, 0, j)),
                  pl.BlockSpec((1, 1, ADA_TN), lambda l, j: (l, 0, j))],
        out_specs=pl.BlockSpec((1, 8, ADA_TN), lambda l, j: (l, 0, j)),
        compiler_params=_cparams(("parallel", "parallel")),
        name="ada",
    )(cond8, w_ada, b_ada.reshape(depth, 1, n9))


FF_CW = 1408


def _ffn_kernel(x_ref, mod_ref, gpre_ref, gpost_ref, wgu_ref, wd_ref, o_ref, act_ref):
    x = x_ref[...]
    m = mod_ref[0]
    h = _modulate(x, gpre_ref[0], m).astype(BF16)
    for c in range(D_FF // FF_CW):
        g = _dot(h, wgu_ref[:, c * FF_CW:(c + 1) * FF_CW])
        u = _dot(h, wgu_ref[:, D_FF + c * FF_CW:D_FF + (c + 1) * FF_CW])
        act_ref[:, c * FF_CW:(c + 1) * FF_CW] = (_silu(g) * u).astype(BF16)
    y = _dot(act_ref[...], wd_ref[...])
    o_ref[...] = _residual(x, y, gpost_ref[0], m, 0.5)


def _ffn(x, mods, ng, wgu, wd, layer, j):
    s = 2 * j
    return pl.pallas_call(
        _ffn_kernel,
        out_shape=jax.ShapeDtypeStruct((N_TOK, D), F32),
        grid=(N_TOK // TM,),
        in_specs=[pl.BlockSpec((TM, D), lambda i: (i, 0)),
                  pl.BlockSpec((1, 1, 3 * D), lambda i: (layer * 3 + _seg_of_tile(i, TM), 0, s)),
                  pl.BlockSpec((1, 1, D), lambda i: (layer * 6 + 2 * s, 0, 0)),
                  pl.BlockSpec((1, 1, D), lambda i: (layer * 6 + 2 * s + 1, 0, 0)),
                  _resident((None, None, D, 2 * D_FF), lambda i: (layer, j, 0, 0)),
                  _resident((None, None, D_FF, D), lambda i: (layer, j, 0, 0))],
        out_specs=pl.BlockSpec((TM, D), lambda i: (i, 0)),
        scratch_shapes=[pltpu.VMEM((TM, D_FF), BF16)],
        compiler_params=_cparams(("parallel",)),
        name="ffn",
    )(x, mods, ng, ng, wgu, wd)


def _proj_kernel(x_ref, mod_ref, g_ref, w_ref, o_ref):
    h = _modulate(x_ref[...], g_ref[0], mod_ref[0]).astype(BF16)
    o_ref[...] = _dot(h, w_ref[...])


def _proj(x, mods, ng, w, layer):
    n = w.shape[1]
    return pl.pallas_call(
        _proj_kernel,
        out_shape=jax.ShapeDtypeStruct((N_TOK, n), F32),
        grid=(N_TOK // TM,),
        in_specs=[pl.BlockSpec((TM, D), lambda i: (i, 0)),
                  pl.BlockSpec((1, 1, 3 * D), lambda i: (layer * 3 + _seg_of_tile(i, TM), 0, 1)),
                  pl.BlockSpec((1, 1, D), lambda i: (layer * 6 + 2, 0, 0)),
                  _resident((D, n), lambda i: (0, 0))],
        out_specs=pl.BlockSpec((TM, n), lambda i: (i, 0)),
        compiler_params=_cparams(("parallel",)),
        name="proj",
    )(x, mods, ng, w)


def _out_pair_kernel(actx_ref, alat_ref, w_ref, x_ref, mod_ref, g_ref, o_ref):
    i = pl.program_id(0)
    a = jnp.where(i < N_CTX // TM, actx_ref[...], alat_ref[...])
    y = _dot(a.astype(BF16), w_ref[...])
    o_ref[...] = _residual(x_ref[...], y, g_ref[0], mod_ref[0], 1.0)


def _out_s5_kernel(yf_ref, yb_ref, w_ref, x_ref, mod_ref, g_ref, o_ref):
    a = jax.nn.gelu(yf_ref[...] + yb_ref[...]).astype(BF16)
    t = _dot(a, w_ref[...])
    y = t[:, :D] * jax.nn.sigmoid(t[:, D:])
    o_ref[...] = _residual(x_ref[...], y, g_ref[0], mod_ref[0], 1.0)


def _out_dn_kernel(of_ref, ob_ref, z_ref, og_ref, w_ref, x_ref, mod_ref, g_ref, o_ref, a_ref):
    o = of_ref[...] + ob_ref[...]
    og = og_ref[...]
    for h in range(DN_V):
        cs = slice(h * DN_HD, (h + 1) * DN_HD)
        oh = o[:, cs]
        on = oh * lax.rsqrt(jnp.mean(oh * oh, axis=-1, keepdims=True) + EPS) * og
        a_ref[:, cs] = (on * _silu(z_ref[:, cs])).astype(BF16)
    y = _dot(a_ref[...], w_ref[...])
    o_ref[...] = _residual(x_ref[...], y, g_ref[0], mod_ref[0], 1.0)


def _tail_specs(layer):
    return [pl.BlockSpec((TM, D), lambda i: (i, 0)),
            pl.BlockSpec((1, 1, 3 * D), lambda i: (layer * 3 + _seg_of_tile(i, TM), 0, 1)),
            pl.BlockSpec((1, 1, D), lambda i: (layer * 6 + 3, 0, 0))]


def _out_pair(actx, alat, w, x, mods, ng, layer):
    nct = N_CTX // TM
    return pl.pallas_call(
        _out_pair_kernel,
        out_shape=jax.ShapeDtypeStruct((N_TOK, D), F32),
        grid=(N_TOK // TM,),
        in_specs=[pl.BlockSpec((TM, D), lambda i: (jnp.minimum(i, nct - 1), 0)),
                  pl.BlockSpec((TM, D), lambda i: (jnp.maximum(i - nct, 0), 0)),
                  _resident((D, D), lambda i: (0, 0))] + _tail_specs(layer),
        out_specs=pl.BlockSpec((TM, D), lambda i: (i, 0)),
        compiler_params=_cparams(("parallel",)),
        name="out_pair",
    )(actx, alat, w, x, mods, ng)


def _out_s5(yf, yb, w, x, mods, ng, layer):
    return pl.pallas_call(
        _out_s5_kernel,
        out_shape=jax.ShapeDtypeStruct((N_TOK, D), F32),
        grid=(N_TOK // TM,),
        in_specs=[pl.BlockSpec((TM, D), lambda i: (i, 0)),
                  pl.BlockSpec((TM, D), lambda i: (i, 0)),
                  _resident((D, 2 * D), lambda i: (0, 0))] + _tail_specs(layer),
        out_specs=pl.BlockSpec((TM, D), lambda i: (i, 0)),
        compiler_params=_cparams(("parallel",)),
        name="out_s5",
    )(yf, yb, w, x, mods, ng)


def _out_dn(of, ob, proj, og, w, x, mods, ng, layer):
    zblk = 2 * DN_NQK + DN_NV
    return pl.pallas_call(
        _out_dn_kernel,
        out_shape=jax.ShapeDtypeStruct((N_TOK, D), F32),
        grid=(N_TOK // TM,),
        in_specs=[pl.BlockSpec((TM, D), lambda i: (i, 0)),
                  pl.BlockSpec((TM, D), lambda i: (i, 0)),
                  pl.BlockSpec((TM, DN_NV), lambda i: (i, zblk // DN_NV)),
                  pl.BlockSpec((1, DN_HD), lambda i: (0, 0)),
                  _resident((DN_NV, D), lambda i: (0, 0))] + _tail_specs(layer),
        out_specs=pl.BlockSpec((TM, D), lambda i: (i, 0)),
        scratch_shapes=[pltpu.VMEM((TM, DN_NV), BF16)],
        compiler_params=_cparams(("parallel",)),
        name="out_dn",
    )(of, ob, proj, og, w, x, mods, ng)


def _softmax_pv(parts, sink):
    m = functools.reduce(jnp.maximum, [jnp.max(s, axis=-1, keepdims=True) for s, _ in parts])
    if sink is not None:
        m = jnp.maximum(m, sink)
    l = None
    o = None
    for s, v in parts:
        p = jnp.exp(s - m)
        ls = jnp.sum(p, axis=-1, keepdims=True)
        os_ = _dot(p.astype(BF16), v)
        l = ls if l is None else l + ls
        o = os_ if o is None else o + os_
    if sink is not None:
        l = l + jnp.exp(sink - m)
    return o / l


def _attn_ctx_kernel(*refs, n_heads, group, has_sink):
    if has_sink:
        sink_ref, q_ref, k_ref, v_ref, o_ref = refs
    else:
        q_ref, k_ref, v_ref, o_ref = refs
    scale = HEAD ** -0.5
    for h in range(n_heads):
        kv = h // group
        q = (q_ref[:, h * HEAD:(h + 1) * HEAD] * scale).astype(BF16)
        k = k_ref[:, kv * HEAD:(kv + 1) * HEAD].astype(BF16)
        v = v_ref[:, kv * HEAD:(kv + 1) * HEAD].astype(BF16)
        s = _dot_nt(q, k)
        o = _softmax_pv([(s, v)], sink_ref[h] if has_sink else None)
        o_ref[:, h * HEAD:(h + 1) * HEAD] = o


def _attn_ctx(qkv, n_heads, n_kv, sink=None):
    qw, kw = n_heads * HEAD, n_kv * HEAD
    kern = functools.partial(_attn_ctx_kernel, n_heads=n_heads, group=n_heads // n_kv,
                             has_sink=sink is not None)
    specs = [pl.BlockSpec((CTX_L, qw), lambda b: (b, 0)),
             pl.BlockSpec((CTX_L, kw), lambda b: (b, qw // kw)),
             pl.BlockSpec((CTX_L, kw), lambda b: (b, qw // kw + 1))]
    args = [qkv, qkv, qkv]
    if sink is not None:
        specs = [pl.BlockSpec(memory_space=pltpu.SMEM)] + specs
        args = [sink] + args
    return pl.pallas_call(
        kern,
        out_shape=jax.ShapeDtypeStruct((N_CTX, qw), F32),
        grid=(CTX_B,),
        in_specs=specs,
        out_specs=pl.BlockSpec((CTX_L, qw), lambda b: (b, 0)),
        compiler_params=_cparams(("parallel",)),
        name="attn_ctx",
    )(*args)


def _rope_tables():
    n = HEAD // 4
    inv = ROPE_BASE ** (-jnp.arange(n, dtype=F32) / n)
    t = jnp.arange(LAT_L)
    ang_r = (t // GRID_W).astype(F32)[:, None] * inv[None, :]
    ang_c = (t % GRID_W).astype(F32)[:, None] * inv[None, :]
    cos = jnp.concatenate([jnp.cos(ang_r), jnp.cos(ang_r), jnp.cos(ang_c), jnp.cos(ang_c)], axis=-1)
    sin = jnp.concatenate([-jnp.sin(ang_r), jnp.sin(ang_r), -jnp.sin(ang_c), jnp.sin(ang_c)], axis=-1)
    return jnp.tile(cos, (1, 2)), jnp.tile(sin, (1, 2))


def _rope(x, cos, sin):
    rows = x.shape[0]
    lane = lax.broadcasted_iota(jnp.int32, (rows, 128), 1)
    first = (lane % 32) < 16
    outs = []
    for cb in range(x.shape[1] // 128):
        xb = x[:, cb * 128:(cb + 1) * 128]
        partner = jnp.where(first, pltpu.roll(xb, 112, 1), pltpu.roll(xb, 16, 1))
        outs.append(xb * cos + partner * sin)
    return jnp.concatenate(outs, axis=1)


A_KWIN = 3 * A_WIN


def _attn_a_lat_kernel(sink_ref, q_ref, k_ref, v_ref, kc_ref, vc_ref, cos_ref, sin_ref, o_ref):
    n = pl.program_id(1)
    nb = LAT_L // A_WIN
    start = pl.multiple_of(jnp.clip(n - 1, 0, nb - 3) * A_WIN, A_WIN)
    q0 = pl.multiple_of(n * A_WIN, A_WIN)
    scale = HEAD ** -0.5
    q = _rope(q_ref[...], cos_ref[pl.ds(q0, A_WIN), :], sin_ref[pl.ds(q0, A_WIN), :]) * scale
    k = _rope(k_ref[pl.ds(start, A_KWIN), :], cos_ref[pl.ds(start, A_KWIN), :],
              sin_ref[pl.ds(start, A_KWIN), :]).astype(BF16)
    v = v_ref[pl.ds(start, A_KWIN), :].astype(BF16)
    kc = kc_ref[0].astype(BF16)
    vc = vc_ref[0].astype(BF16)
    qpos = q0 + lax.broadcasted_iota(jnp.int32, (A_WIN, A_KWIN), 0)
    kpos = start + lax.broadcasted_iota(jnp.int32, (A_WIN, A_KWIN), 1)
    ok = jnp.abs(kpos - qpos) <= A_WIN
    group = A_HEADS // A_KV
    for h in range(A_HEADS):
        kvs = slice((h // group) * HEAD, (h // group + 1) * HEAD)
        qh = q[:, h * HEAD:(h + 1) * HEAD].astype(BF16)
        s_loc = jnp.where(ok, _dot_nt(qh, k[:, kvs]), NEG)
        s_ctx = _dot_nt(qh, kc[:, kvs])
        o = _softmax_pv([(s_loc, v[:, kvs]), (s_ctx, vc[:, kvs])], sink_ref[h])
        o_ref[:, h * HEAD:(h + 1) * HEAD] = o


def _attn_a_lat(qkv, sink, kc, vc, cos, sin):
    qw, kw = A_HEADS * HEAD, A_KV * HEAD
    nb = LAT_L // A_WIN
    return pl.pallas_call(
        _attn_a_lat_kernel,
        out_shape=jax.ShapeDtypeStruct((N_LAT, qw), F32),
        grid=(LAT_B, nb),
        in_specs=[pl.BlockSpec(memory_space=pltpu.SMEM),
                  pl.BlockSpec((A_WIN, qw), lambda b, n: (N_CTX // A_WIN + b * nb + n, 0)),
                  pl.BlockSpec((LAT_L, kw), lambda b, n: (N_CTX // LAT_L + b, qw // kw)),
                  pl.BlockSpec((LAT_L, kw), lambda b, n: (N_CTX // LAT_L + b, qw // kw + 1)),
                  pl.BlockSpec((1, PAST, kw), lambda b, n: (b, 0, 0)),
                  pl.BlockSpec((1, PAST, kw), lambda b, n: (b, 0, 0)),
                  _resident((LAT_L, 128), lambda b, n: (0, 0)),
                  _resident((LAT_L, 128), lambda b, n: (0, 0))],
        out_specs=pl.BlockSpec((A_WIN, qw), lambda b, n: (b * nb + n, 0)),
        compiler_params=_cparams(("parallel", "parallel")),
        name="attn_a_lat",
    )(sink, qkv, qkv, qkv, kc, vc, cos, sin)


NA_KEYS = NA_ROWS * GRID_W


def _na_bias_table(rpb):
    col = jnp.arange(GRID_W)
    dcol = jnp.clip(col[None, :] - col[:, None], 1 - NA_COLS, NA_COLS - 1) + NA_COLS - 1
    cs = jnp.clip(col - NA_COLS // 2, 0, GRID_W - NA_COLS)
    ok = (col[None, :] >= cs[:, None]) & (col[None, :] < cs[:, None] + NA_COLS)
    t = jnp.where(ok[None, None], rpb[:, :, dcol], NEG)
    idx = jnp.arange(NA_ROWS)[:, None] + jnp.arange(NA_ROWS)[None, :]
    bt = t[:, idx]
    return jnp.transpose(bt, (1, 0, 3, 2, 4)).reshape(NA_ROWS, C_HEADS, GRID_W, NA_KEYS)


def _attn_c_lat_kernel(q_ref, k_ref, v_ref, kc_ref, vc_ref, bias_ref, o_ref):
    r = pl.program_id(1)
    rows = LAT_L // GRID_W
    start = pl.multiple_of(jnp.clip(r - NA_ROWS // 2, 0, rows - NA_ROWS) * GRID_W, GRID_W)
    scale = HEAD ** -0.5
    q = q_ref[...] * scale
    for h in range(C_HEADS):
        hs = slice(h * HEAD, (h + 1) * HEAD)
        qh = q[:, hs].astype(BF16)
        s_loc = _dot_nt(qh, k_ref[pl.ds(start, NA_KEYS), hs]) + bias_ref[h]
        s_ctx = _dot_nt(qh, kc_ref[0, :, hs].astype(BF16))
        o = _softmax_pv([(s_loc, v_ref[pl.ds(start, NA_KEYS), hs]),
                         (s_ctx, vc_ref[0, :, hs].astype(BF16))], None)
        o_ref[:, hs] = o


def _attn_c_lat(qkv, kv_bf, kc, vc, bias):
    qw = C_HEADS * HEAD
    rows = LAT_L // GRID_W

    def bias_idx(b, r):
        return (jnp.clip(r - NA_ROWS // 2, 0, rows - NA_ROWS) - r + NA_ROWS - 1, 0, 0, 0)

    return pl.pallas_call(
        _attn_c_lat_kernel,
        out_shape=jax.ShapeDtypeStruct((N_LAT, qw), F32),
        grid=(LAT_B, rows),
        in_specs=[pl.BlockSpec((GRID_W, qw), lambda b, r: (N_CTX // GRID_W + b * rows + r, 0)),
                  _resident((LAT_L, qw), lambda b, r: (b, 0)),
                  _resident((LAT_L, qw), lambda b, r: (b, 1)),
                  pl.BlockSpec((1, PAST, qw), lambda b, r: (b, 0, 0)),
                  pl.BlockSpec((1, PAST, qw), lambda b, r: (b, 0, 0)),
                  pl.BlockSpec((None, C_HEADS, GRID_W, NA_KEYS), bias_idx)],
        out_specs=pl.BlockSpec((GRID_W, qw), lambda b, r: (b * rows + r, 0)),
        compiler_params=_cparams(("parallel", "arbitrary")),
        name="attn_c_lat",
    )(qkv, kv_bf, kv_bf, kc, vc, bias)


def _seq_tables():
    fb, bb, first, last, sid = [], [], [], [], []
    base = 0
    for s, length in enumerate([CTX_L] * CTX_B + [LAT_L] * LAT_B):
        n = length // SEQ_BLK
        for c in range(n):
            fb.append(base + c)
            bb.append(base + n - 1 - c)
            first.append(int(c == 0))
            last.append(int(c == n - 1))
            sid.append(s)
        base += n
    return tuple(jnp.asarray(np.array(t, np.int32)) for t in (fb, bb, first, last, sid))


def _seg_of_sid(s):
    return jnp.maximum(s - (CTX_B - 1), 0)


S5_GB = 8
S5_LB = 1024
S5_TILES = SEQ_BLK // 8


def _s5_tables(lam_re, lam_im, log_dt, b_re, b_im, c_re, c_im):
    dt = jnp.exp(log_dt)[..., None]
    lr, li = lam_re * dt, lam_im * dt
    a_re, a_im = jnp.exp(lr) * jnp.cos(li), jnp.exp(lr) * jnp.sin(li)
    den = lam_re * lam_re + lam_im * lam_im
    fr = ((a_re - 1.0) * lam_re + a_im * lam_im) / den
    fi = (a_im * lam_re - (a_re - 1.0) * lam_im) / den
    bb_re = fr[..., None] * b_re - fi[..., None] * b_im
    bb_im = fr[..., None] * b_im + fi[..., None] * b_re
    eye = jnp.eye(S5_GB, dtype=F32)

    def bdiag_in(t):
        t = t.reshape(2, S5_GROUPS // S5_GB, S5_GB, S5_STATE, S5_GROUP)
        return jnp.einsum('dbgpc,gh->dbgchp', t, eye).reshape(2, S5_GROUPS // S5_GB, 128, 512)

    def bdiag_out(t):
        t = t.reshape(2, S5_GROUPS // S5_GB, S5_GB, S5_GROUP, S5_STATE)
        return jnp.einsum('dbgcp,gh->dbgphc', t, eye).reshape(2, S5_GROUPS // S5_GB, 512, 128)

    wb = jnp.stack([bdiag_in(bb_re), bdiag_in(bb_im)], axis=1).astype(BF16)
    wc = jnp.stack([bdiag_out(c_re), bdiag_out(c_im)], axis=1).astype(BF16)

    ar, ai = a_re.reshape(2, S5_N), a_im.reshape(2, S5_N)
    pows = [(ar, ai)]
    for _ in range(7):
        pr, pi = pows[-1]
        pows.append((pr * ar - pi * ai, pr * ai + pi * ar))
    row = jnp.arange(8)[:, None]
    zero = jnp.zeros((8, S5_N), F32)
    ta, pt = [], []
    for d in range(2):
        steps = []
        for s in (1, 2, 4):
            keep = (row >= s) if d == 0 else (row + s <= 7)
            steps.append(jnp.stack([jnp.where(keep, pows[s - 1][ri][d][None, :], zero) for ri in range(2)]))
        ta.append(jnp.stack(steps))
        order = list(range(8)) if d == 0 else list(range(7, -1, -1))
        pt.append(jnp.stack([jnp.stack([pows[k][ri][d] for k in order]) for ri in range(2)]))
    return wb, wc, jnp.stack(ta), jnp.stack(pt)


def _s5_kernel(fb_ref, bb_ref, first_ref, last_ref, sid_ref,
               xf_ref, xb_ref, mod_ref, g_ref, dsk_ref, wb_ref, wc_ref, ta_ref, pt_ref, sre_ref, sim_ref,
               yf_ref, yb_ref, ore_ref, oim_ref, bur, bui, car):
    i = pl.program_id(0)
    is_first = first_ref[i] == 1
    is_lat = sid_ref[i] >= CTX_B

    @pl.when(is_first & is_lat)
    def _():
        car[:, 0] = sre_ref[0]
        car[:, 1] = sim_ref[0]

    @pl.when(is_first & jnp.logical_not(is_lat))
    def _():
        car[...] = jnp.zeros_like(car)

    for d in range(2):
        x_ref, y_ref = (xf_ref, yf_ref) if d == 0 else (xb_ref, yb_ref)
        hm = _modulate(x_ref[...], g_ref[0], mod_ref[0])
        hb = hm.astype(BF16)
        for gb in range(S5_GROUPS // S5_GB):
            ub = hb[:, gb * 128:(gb + 1) * 128]
            bur[:, gb * 512:(gb + 1) * 512] = _dot(ub, wb_ref[d, 0, gb])
            bui[:, gb * 512:(gb + 1) * 512] = _dot(ub, wb_ref[d, 1, gb])

        last_row = 7 if d == 0 else 0
        for lb in range(S5_N // S5_LB):
            cs = slice(lb * S5_LB, (lb + 1) * S5_LB)

            def body(t, carry, d=d, cs=cs, last_row=last_row):
                cr, ci = carry
                tt = t if d == 0 else S5_TILES - 1 - t
                r0 = pl.multiple_of(tt * 8, 8)
                xr = bur[pl.ds(r0, 8), cs]
                xi = bui[pl.ds(r0, 8), cs]
                for si, s in enumerate((1, 2, 4)):
                    sh = s if d == 0 else 8 - s
                    sr = pltpu.roll(xr, sh, 0)
                    sm = pltpu.roll(xi, sh, 0)
                    ar = ta_ref[d, si, 0, :, cs]
                    ai = ta_ref[d, si, 1, :, cs]
                    xr, xi = xr + ar * sr - ai * sm, xi + ar * sm + ai * sr
                pr = pt_ref[d, 0, :, cs]
                pi = pt_ref[d, 1, :, cs]
                xr, xi = xr + pr * cr - pi * ci, xi + pr * ci + pi * cr
                bur[pl.ds(r0, 8), cs] = xr
                bui[pl.ds(r0, 8), cs] = xi
                return xr[last_row:last_row + 1], xi[last_row:last_row + 1]

            cr, ci = lax.fori_loop(0, S5_TILES, body, (car[d, 0, :, cs], car[d, 1, :, cs]))
            car[d, 0, :, cs] = cr
            car[d, 1, :, cs] = ci

        for gb in range(S5_GROUPS // S5_GB):
            xr = bur[:, gb * 512:(gb + 1) * 512].astype(BF16)
            xi = bui[:, gb * 512:(gb + 1) * 512].astype(BF16)
            y = _dot(xr, wc_ref[d, 0, gb]) - _dot(xi, wc_ref[d, 1, gb])
            if d == 0:
                y = y + dsk_ref[:, gb * 128:(gb + 1) * 128] * hm[:, gb * 128:(gb + 1) * 128]
            y_ref[:, gb * 128:(gb + 1) * 128] = y

    ore_ref[0] = car[:, 0]
    oim_ref[0] = car[:, 1]


def _s5(x, mods, ng, dskip, tabs, state_re, state_im, layer, tables):
    wb, wc, ta, pt = tabs
    n_steps = tables[0].shape[0]
    n_seq = CTX_B + LAT_B

    def mod_idx(i, fb, bb, first, last, sid):
        return (layer * 3 + _seg_of_sid(sid[i]), 0, 1)

    def st_idx(i, fb, bb, first, last, sid):
        return (jnp.maximum(sid[i] - CTX_B, 0), 0, 0, 0)

    def const(nd):
        return lambda i, *_: (0,) * nd

    grid_spec = pltpu.PrefetchScalarGridSpec(
        num_scalar_prefetch=5,
        grid=(n_steps,),
        in_specs=[pl.BlockSpec((SEQ_BLK, D), lambda i, fb, *_: (fb[i], 0)),
                  pl.BlockSpec((SEQ_BLK, D), lambda i, fb, bb, *_: (bb[i], 0)),
                  pl.BlockSpec((1, 1, 3 * D), mod_idx),
                  pl.BlockSpec((1, 1, D), lambda i, *_: (layer * 6 + 2, 0, 0)),
                  pl.BlockSpec((1, D), const(2)),
                  _resident(wb.shape, const(5)),
                  _resident(wc.shape, const(5)),
                  _resident(ta.shape, const(5)),
                  _resident(pt.shape, const(4)),
                  pl.BlockSpec((1, 2, 1, S5_N), st_idx),
                  pl.BlockSpec((1, 2, 1, S5_N), st_idx)],
        out_specs=[pl.BlockSpec((SEQ_BLK, D), lambda i, fb, *_: (fb[i], 0)),
                   pl.BlockSpec((SEQ_BLK, D), lambda i, fb, bb, *_: (bb[i], 0)),
                   pl.BlockSpec((1, 2, 1, S5_N), lambda i, fb, bb, first, last, sid: (sid[i], 0, 0, 0)),
                   pl.BlockSpec((1, 2, 1, S5_N), lambda i, fb, bb, first, last, sid: (sid[i], 0, 0, 0))],
        scratch_shapes=[pltpu.VMEM((SEQ_BLK, S5_N), F32), pltpu.VMEM((SEQ_BLK, S5_N), F32),
                        pltpu.VMEM((2, 2, 1, S5_N), F32)])
    return pl.pallas_call(
        _s5_kernel,
        out_shape=[jax.ShapeDtypeStruct((N_TOK, D), F32), jax.ShapeDtypeStruct((N_TOK, D), F32),
                   jax.ShapeDtypeStruct((n_seq, 2, 1, S5_N), F32), jax.ShapeDtypeStruct((n_seq, 2, 1, S5_N), F32)],
        grid_spec=grid_spec,
        compiler_params=_cparams(("arbitrary",)),
        name="s5",
    )(*tables, x, x, mods, ng, dskip, wb, wc, ta, pt, state_re, state_im)


def _dn_prep_kernel(blk_ref, first_ref, last_ref, x_ref, prev_ref, next_ref, ba_ref, cw_ref, alog_ref, dtb_ref,
                    q_ref, k_ref, v_ref, g_ref):
    i = pl.program_id(0)
    nqkv = 2 * DN_NQK + DN_NV
    x = x_ref[...]
    pv = jnp.where(first_ref[i] == 1, 0.0, prev_ref[...])
    nx = jnp.where(last_ref[i] == 1, 0.0, next_ref[...])
    ext = jnp.concatenate([pv, x, nx], axis=0)
    n_ext = SEQ_BLK + 16
    acc = None
    for j in range(DN_CONV):
        off = DN_CONV // 2 - j
        e = ext if off == 0 else pltpu.roll(ext, off % n_ext, 0)
        term = cw_ref[j:j + 1, :] * e[8:8 + SEQ_BLK]
        acc = term if acc is None else acc + term
    a = _silu(acc)
    for h in range(DN_QK):
        cs = slice(h * DN_HD, (h + 1) * DN_HD)
        qh = a[:, cs]
        q_ref[:, cs] = qh * lax.rsqrt(jnp.sum(qh * qh, axis=-1, keepdims=True) + EPS) * (DN_HD ** -0.5)
        kh = a[:, DN_NQK + h * DN_HD:DN_NQK + (h + 1) * DN_HD]
        k_ref[:, cs] = kh * lax.rsqrt(jnp.sum(kh * kh, axis=-1, keepdims=True) + EPS)
    v_ref[...] = a[:, 2 * DN_NQK:nqkv]

    ba = ba_ref[...]
    lane = lax.broadcasted_iota(jnp.int32, ba.shape, 1)
    row = lax.broadcasted_iota(jnp.int32, ba.shape, 0) % DN_CHUNK
    is_g = ((lane % 16) >= 8) & (lane < 32)
    z = ba + dtb_ref[...]
    softplus = jnp.maximum(z, 0.0) + jnp.log1p(jnp.exp(-jnp.abs(z)))
    val = jnp.where(is_g, -jnp.exp(alog_ref[...]) * softplus, jax.nn.sigmoid(ba))
    cf = val
    cr = val
    s = 1
    while s < DN_CHUNK:
        cf = cf + jnp.where(row >= s, pltpu.roll(cf, s, 0), 0.0)
        cr = cr + jnp.where(row < DN_CHUNK - s, pltpu.roll(cr, SEQ_BLK - s, 0), 0.0)
        s *= 2
    g_ref[...] = jnp.where(lane < 32, jnp.where(is_g, jnp.where(lane < 16, cf, cr), val), 0.0)


def _dn_prep(proj, conv_w, alog_row, dtb_row, tables):
    fb, _, first, last, _ = tables
    nqkv = 2 * DN_NQK + DN_NV
    per = SEQ_BLK // 8
    n8 = N_TOK // 8
    grid_spec = pltpu.PrefetchScalarGridSpec(
        num_scalar_prefetch=3,
        grid=(fb.shape[0],),
        in_specs=[pl.BlockSpec((SEQ_BLK, nqkv), lambda i, blk, *_: (blk[i], 0)),
                  pl.BlockSpec((8, nqkv), lambda i, blk, *_: (jnp.maximum(blk[i] * per - 1, 0), 0)),
                  pl.BlockSpec((8, nqkv), lambda i, blk, *_: (jnp.minimum(blk[i] * per + per, n8 - 1), 0)),
                  pl.BlockSpec((SEQ_BLK, 128), lambda i, blk, *_: (blk[i], (nqkv + DN_NV) // 128)),
                  pl.BlockSpec((DN_CONV, nqkv), lambda i, *_: (0, 0)),
                  pl.BlockSpec((1, 128), lambda i, *_: (0, 0)),
                  pl.BlockSpec((1, 128), lambda i, *_: (0, 0))],
        out_specs=[pl.BlockSpec((SEQ_BLK, DN_NQK), lambda i, blk, *_: (blk[i], 0)),
                   pl.BlockSpec((SEQ_BLK, DN_NQK), lambda i, blk, *_: (blk[i], 0)),
                   pl.BlockSpec((SEQ_BLK, DN_NV), lambda i, blk, *_: (blk[i], 0)),
                   pl.BlockSpec((SEQ_BLK, 128), lambda i, blk, *_: (blk[i], 0))])
    return pl.pallas_call(
        _dn_prep_kernel,
        out_shape=[jax.ShapeDtypeStruct((N_TOK, DN_NQK), F32), jax.ShapeDtypeStruct((N_TOK, DN_NQK), F32),
                   jax.ShapeDtypeStruct((N_TOK, DN_NV), F32), jax.ShapeDtypeStruct((N_TOK, 128), F32)],
        grid_spec=grid_spec,
        compiler_params=_cparams(("arbitrary",)),
        name="dn_prep",
    )(fb, first, last, proj, proj, proj, proj, conv_w, alog_row, dtb_row)


DN_NCH = SEQ_BLK // DN_CHUNK


DN_LEVELS = int(math.log2(DN_CHUNK))


def _dn_setup(d, lane_beta, q, k, kk, qk, v, gates):
    n = SEQ_BLK
    lane = lax.broadcasted_iota(jnp.int32, gates.shape, 1)
    beta = jnp.sum(jnp.where(lane == lane_beta, gates, 0.0), axis=1, keepdims=True)
    gcol = jnp.sum(jnp.where(lane == lane_beta + 8, gates, 0.0), axis=1, keepdims=True)
    ri = lax.broadcasted_iota(jnp.int32, (n, n), 0)
    ci = lax.broadcasted_iota(jnp.int32, (n, n), 1)
    eye = ri == ci
    same = (ri // DN_CHUNK) == (ci // DN_CHUNK)
    grow = jnp.sum(jnp.where(eye, gcol, 0.0), axis=0, keepdims=True)
    if d == 0:
        lower, strict = same & (ri >= ci), same & (ri > ci)
        last_of = (ri // DN_CHUNK) * DN_CHUNK + DN_CHUNK - 1
    else:
        lower, strict = same & (ri <= ci), same & (ri < ci)
        last_of = (ri // DN_CHUNK) * DN_CHUNK
    decay = jnp.exp(jnp.where(lower, gcol - grow, NEG))
    g_last = jnp.sum(jnp.where(ci == last_of, grow, 0.0), axis=1, keepdims=True)
    return dict(
        d=d,
        lmat=jnp.where(strict, beta * kk * decay, 0.0),
        rhs16=jnp.concatenate([v * beta, k * (beta * jnp.exp(gcol))], axis=1).astype(BF16),
        aqk16=jnp.where(lower, qk * decay, 0.0).astype(BF16),
        qe16=(q * jnp.exp(gcol)).astype(BF16),
        kd16=(k * jnp.exp(g_last - gcol)).astype(BF16),
        eg=jnp.exp(g_last))


def _dn_solve_all(chains):
    n = SEQ_BLK
    ri = lax.broadcasted_iota(jnp.int32, (n, n), 0)
    ci = lax.broadcasted_iota(jnp.int32, (n, n), 1)
    diff = ri ^ ci
    level = sum((diff >= (1 << b)).astype(jnp.int32) for b in range(DN_LEVELS + 1))
    ts = [jnp.where(ri == ci, 1.0, 0.0) - jnp.where(level == 1, c["lmat"], 0.0) for c in chains]
    for lv in range(2, DN_LEVELS + 1):
        t16 = [t.astype(BF16) for t in ts]
        xs = [_dot(jnp.where(level == lv, c["lmat"], 0.0).astype(BF16), t) for c, t in zip(chains, t16)]
        ts = [t - _dot(th, x.astype(BF16)) for t, th, x in zip(ts, t16, xs)]
    return [_dot(t.astype(BF16), c["rhs16"]) for c, t in zip(chains, ts)]


def _dn_scan_all(chains, sols, states):
    us = [s[:, :DN_HD] for s in sols]
    w16 = [s[:, DN_HD:].astype(BF16) for s in sols]
    outs = [[None] * DN_NCH for _ in chains]
    for step in range(DN_NCH):
        idx = [step if c["d"] == 0 else DN_NCH - 1 - step for c in chains]
        rows = [slice(i * DN_CHUNK, (i + 1) * DN_CHUNK) for i in idx]
        s16 = [s.astype(BF16) for s in states]
        vn16 = [(u[r] - _dot(w[r], s)).astype(BF16) for u, w, r, s in zip(us, w16, rows, s16)]
        for n_, (c, r, s, vn) in enumerate(zip(chains, rows, s16, vn16)):
            outs[n_][idx[n_]] = _dot(c["qe16"][r], s) + _dot(c["aqk16"][r, r], vn)
        states = [s * c["eg"][r.start:r.start + 1] + _dot_tn(c["kd16"][r], vn)
                  for s, c, r, vn in zip(states, chains, rows, vn16)]
    return [jnp.concatenate(o, axis=0) for o in outs], states


def _dn_kernel(fb_ref, bb_ref, first_ref, last_ref, sid_ref,
               qf_ref, kf_ref, vf_ref, gf_ref, qb_ref, kb_ref, vb_ref, gb_ref, s0_ref,
               of_ref, ob_ref, so_ref, s_ref):
    j = pl.program_id(0)
    i = pl.program_id(1)
    is_first = first_ref[i] == 1
    is_lat = sid_ref[i] >= CTX_B

    @pl.when(is_first & is_lat)
    def _():
        s_ref[...] = s0_ref[0]

    @pl.when(is_first & jnp.logical_not(is_lat))
    def _():
        s_ref[...] = jnp.zeros_like(s_ref)

    chains = []
    for d in range(2):
        q_ref, k_ref, v_ref, g_ref = (qf_ref, kf_ref, vf_ref, gf_ref) if d == 0 else (qb_ref, kb_ref, vb_ref, gb_ref)
        q, k, gates = q_ref[...], k_ref[...], g_ref[...]
        k16 = k.astype(BF16)
        kk = _dot_nt(k16, k16)
        qk = _dot_nt(q.astype(BF16), k16)
        for e in range(DN_REP):
            chains.append(_dn_setup(d, 16 * d + DN_REP * j + e, q, k, kk, qk,
                                    v_ref[:, e * DN_HD:(e + 1) * DN_HD], gates))
    sols = _dn_solve_all(chains)
    outs, states = _dn_scan_all(chains, sols, [s_ref[d, e] for d in range(2) for e in range(DN_REP)])
    for n_, (o, s_new) in enumerate(zip(outs, states)):
        d, e = divmod(n_, DN_REP)
        (of_ref if d == 0 else ob_ref)[:, e * DN_HD:(e + 1) * DN_HD] = o
        s_ref[d, e] = s_new
    so_ref[0] = s_ref[...]


def _dn(qn, kn, vv, gates, state, tables):
    n_steps = tables[0].shape[0]
    n_seq = CTX_B + LAT_B

    def fwd(col):
        return lambda j, i, fb, *_: (fb[i], col(j))

    def bwd(col):
        return lambda j, i, fb, bb, *_: (bb[i], col(j))

    head = lambda j: j
    gate = lambda j: 0

    def st_in(j, i, fb, bb, first, last, sid):
        return (jnp.maximum(sid[i] - CTX_B, 0), 0, j, 0, 0)

    def st_out(j, i, fb, bb, first, last, sid):
        return (sid[i], 0, j, 0, 0)

    blk = lambda w: (SEQ_BLK, w)
    vw = DN_REP * DN_HD
    grid_spec = pltpu.PrefetchScalarGridSpec(
        num_scalar_prefetch=5,
        grid=(DN_QK, n_steps),
        in_specs=[pl.BlockSpec(blk(DN_HD), fwd(head)), pl.BlockSpec(blk(DN_HD), fwd(head)),
                  pl.BlockSpec(blk(vw), fwd(head)), pl.BlockSpec(blk(128), fwd(gate)),
                  pl.BlockSpec(blk(DN_HD), bwd(head)), pl.BlockSpec(blk(DN_HD), bwd(head)),
                  pl.BlockSpec(blk(vw), bwd(head)), pl.BlockSpec(blk(128), bwd(gate)),
                  pl.BlockSpec((1, 2, DN_REP, DN_HD, DN_HD), st_in)],
        out_specs=[pl.BlockSpec(blk(vw), fwd(head)), pl.BlockSpec(blk(vw), bwd(head)),
                   pl.BlockSpec((1, 2, DN_REP, DN_HD, DN_HD), st_out)],
        scratch_shapes=[pltpu.VMEM((2, DN_REP, DN_HD, DN_HD), F32)])
    return pl.pallas_call(
        _dn_kernel,
        out_shape=[jax.ShapeDtypeStruct((N_TOK, DN_NV), F32), jax.ShapeDtypeStruct((N_TOK, DN_NV), F32),
                   jax.ShapeDtypeStruct((n_seq, 2, DN_V, DN_HD, DN_HD), F32)],
        grid_spec=grid_spec,
        compiler_params=_cparams(("arbitrary", "arbitrary")),
        name="dn",
    )(*tables, qn, kn, vv, gates, qn, kn, vv, gates, state)


def kernel(x_prompt, x_sample, cache_attn_k, cache_attn_v, state_s5_re, state_s5_im, cache_na_k, cache_na_v,
           state_dn, c, c_ctx, norm_g, w_ada, b_ada, ffn_w_gu, ffn_w_d, a_w_qkv, a_w_o, a_sink,
           s5_lam_re, s5_lam_im, s5_log_dt, s5_b_re, s5_b_im, s5_c_re, s5_c_im, s5_d, s5_w_glu,
           na_w_qkv, na_w_o, na_rpb, dn_w_in, dn_conv_w, dn_w_ba, dn_a_log, dn_dt_bias, dn_out_g, dn_w_o):
    depth = w_ada.shape[0]
    x = jnp.concatenate([x_prompt.reshape(N_CTX, D), x_sample.reshape(N_LAT, D)], axis=0)
    cond8 = jnp.concatenate([c_ctx[None, :], c, jnp.zeros((8 - 1 - LAT_B, D), F32)], axis=0)
    mods = _ada(cond8, w_ada, b_ada)[:, :1 + LAT_B].reshape(depth * (1 + LAT_B), 1, 9 * D)
    ng = norm_g.reshape(depth * 6, 1, D)
    wgu = ffn_w_gu.astype(BF16)
    wd = ffn_w_d.astype(BF16)
    tables = _seq_tables()

    x = _ffn(x, mods, ng, wgu, wd, 0, 0)
    qkv = _proj(x, mods, ng, a_w_qkv[0].astype(BF16), 0)
    qw, kw = A_HEADS * HEAD, A_KV * HEAD
    cos, sin = _rope_tables()
    o_ctx = _attn_ctx(qkv, A_HEADS, A_KV, a_sink[0])
    o_lat = _attn_a_lat(qkv, a_sink[0], cache_attn_k[:, 0].reshape(LAT_B, PAST, kw),
                        cache_attn_v[:, 0].reshape(LAT_B, PAST, kw), cos, sin)
    new_attn_k = qkv[:N_CTX, qw:qw + kw].reshape(CTX_B, 1, CTX_L, A_KV, HEAD)
    new_attn_v = qkv[:N_CTX, qw + kw:].reshape(CTX_B, 1, CTX_L, A_KV, HEAD)
    x = _out_pair(o_ctx, o_lat, a_w_o[0].astype(BF16), x, mods, ng, 0)
    x = _ffn(x, mods, ng, wgu, wd, 0, 1)

    x = _ffn(x, mods, ng, wgu, wd, 1, 0)
    tabs = _s5_tables(s5_lam_re[0], s5_lam_im[0], s5_log_dt[0], s5_b_re[0], s5_b_im[0], s5_c_re[0], s5_c_im[0])
    yf, yb, fin_re, fin_im = _s5(x, mods, ng, s5_d, tabs,
                                 state_s5_re[:, 0].reshape(LAT_B, 2, 1, S5_N),
                                 state_s5_im[:, 0].reshape(LAT_B, 2, 1, S5_N), 1, tables)
    new_s5_re = fin_re[:CTX_B].reshape(CTX_B, 1, 2, S5_GROUPS, S5_STATE)
    new_s5_im = fin_im[:CTX_B].reshape(CTX_B, 1, 2, S5_GROUPS, S5_STATE)
    x = _out_s5(yf, yb, s5_w_glu[0].astype(BF16), x, mods, ng, 1)
    x = _ffn(x, mods, ng, wgu, wd, 1, 1)

    x = _ffn(x, mods, ng, wgu, wd, 2, 0)
    qkv = _proj(x, mods, ng, na_w_qkv[0].astype(BF16), 2)
    cw = C_HEADS * HEAD
    o_ctx = _attn_ctx(qkv, C_HEADS, C_HEADS)
    o_lat = _attn_c_lat(qkv, qkv[N_CTX:, cw:].astype(BF16), cache_na_k[:, 0].reshape(LAT_B, PAST, cw),
                        cache_na_v[:, 0].reshape(LAT_B, PAST, cw), _na_bias_table(na_rpb[0]))
    new_na_k = qkv[:N_CTX, cw:2 * cw].reshape(CTX_B, 1, CTX_L, C_HEADS, HEAD)
    new_na_v = qkv[:N_CTX, 2 * cw:].reshape(CTX_B, 1, CTX_L, C_HEADS, HEAD)
    x = _out_pair(o_ctx, o_lat, na_w_o[0].astype(BF16), x, mods, ng, 2)
    x = _ffn(x, mods, ng, wgu, wd, 2, 1)

    x = _ffn(x, mods, ng, wgu, wd, 3, 0)
    w_all = jnp.concatenate([dn_w_in[0], dn_w_ba[0, 0], dn_w_ba[0, 1],
                             jnp.zeros((D, 128 - 4 * DN_V), F32)], axis=1).astype(BF16)
    proj = _proj(x, mods, ng, w_all, 3)
    pad8 = jnp.zeros((DN_V,), F32)
    gate_row = lambda t: jnp.concatenate([pad8, t[0], pad8, t[1], jnp.zeros((128 - 4 * DN_V,), F32)])[None, :]
    qn, kn, vv, gates = _dn_prep(proj, dn_conv_w[0], gate_row(dn_a_log[0]), gate_row(dn_dt_bias[0]), tables)
    of, ob, fin_dn = _dn(qn, kn, vv, gates, state_dn[:, 0], tables)
    new_dn = fin_dn[:CTX_B][:, None]
    x = _out_dn(of, ob, proj, dn_out_g, dn_w_o[0].astype(BF16), x, mods, ng, 3)
    x = _ffn(x, mods, ng, wgu, wd, 3, 1)

    y = x[:N_CTX].reshape(CTX_B, CTX_L, D)
    z = x[N_CTX:].reshape(LAT_B, LAT_L, D)
    return (y, z, new_attn_k, new_attn_v, new_s5_re, new_s5_im, new_na_k, new_na_v, new_dn)
```

```python
import functools
import math

import numpy as np
import jax
import jax.numpy as jnp
from jax import lax
from jax.experimental import pallas as pl
from jax.experimental.pallas import tpu as pltpu

F32 = jnp.float32
BF16 = jnp.bfloat16

D = 1024
CTX_B, CTX_L = 16, 256
LAT_B, LAT_L = 2, 4096
N_CTX = CTX_B * CTX_L
N_LAT = LAT_B * LAT_L
N_TOK = N_CTX + N_LAT
PAST = 512
D_FF = 2816
EPS = 1e-6
NEG = -1e30
GRID_W = 64
HEAD = 64
A_HEADS, A_KV = 16, 4
A_WIN = 128
C_HEADS = 16
NA_ROWS, NA_COLS = 8, 16
ROPE_BASE = 10000.0
S5_GROUPS, S5_GROUP, S5_STATE = 64, 16, 64
S5_N = S5_GROUPS * S5_STATE
DN_QK, DN_V, DN_HD, DN_CONV, DN_CHUNK = 4, 8, 128, 5, 64
DN_REP = DN_V // DN_QK
DN_NQK = DN_QK * DN_HD
DN_NV = DN_V * DN_HD
DN_PROJ = 2 * DN_NQK + 2 * DN_NV + 128

TM = 512
SEQ_BLK = 256
VMEM_MB = 56
ADA_TN = 1152


def _cparams(sem, mb=VMEM_MB):
    return pltpu.CompilerParams(dimension_semantics=sem, vmem_limit_bytes=mb * 1024 * 1024)


def _resident(block, index_map):
    return pl.BlockSpec(block, index_map, pipeline_mode=pl.Buffered(1))


def _seg_of_tile(i, tm):
    nct = N_CTX // tm
    return jnp.where(i < nct, 0, 1 + (i - nct) // (LAT_L // tm))


def _dot(a, b):
    return jnp.dot(a, b, preferred_element_type=F32)


def _dot_nt(a, b):
    return lax.dot_general(a, b, (((1,), (1,)), ((), ())), preferred_element_type=F32)


def _dot_tn(a, b):
    return lax.dot_general(a, b, (((0,), (0,)), ((), ())), preferred_element_type=F32)


def _silu(x):
    return x * jax.nn.sigmoid(x)


def _modulate(x, g, m):
    xn = x * lax.rsqrt(jnp.mean(x * x, axis=-1, keepdims=True) + EPS) * g
    return xn * (1.0 + m[:, D:2 * D]) + m[:, :D]


def _residual(x, y, g, m, weight):
    yn = y * lax.rsqrt(jnp.mean(y * y, axis=-1, keepdims=True) + EPS) * g
    return x + weight * m[:, 2 * D:] * yn


def _ada_kernel(cond_ref, w_ref, b_ref, o_ref):
    s = _silu(cond_ref[...]).astype(BF16)
    o_ref[0] = _dot(s, w_ref[0].astype(BF16)) + b_ref[0]


def _ada(cond8, w_ada, b_ada):
    depth, _, n9 = w_ada.shape
    return pl.pallas_call(
        _ada_kernel,
        out_shape=jax.ShapeDtypeStruct((depth, 8, n9), F32),
        grid=(depth, n9 // ADA_TN),
        in_specs=[pl.BlockSpec((8, D), lambda l, j: (0, 0)),
                  pl.BlockSpec((1, D, ADA_TN), lambda l, j: (l, 0, j)),
                  pl.BlockSpec((1, 1, ADA_TN), lambda l, j: (l, 0, j))],
        out_specs=pl.BlockSpec((1, 8, ADA_TN), lambda l, j: (l, 0, j)),
        compiler_params=_cparams(("parallel", "parallel")),
        name="ada",
    )(cond8, w_ada, b_ada.reshape(depth, 1, n9))


FF_CW = 1408


def _ffn_kernel(*refs, n_x, n_out):
    x_refs, (mod_ref, gpre_ref, gpost_ref, wgu_ref, wd_ref) = refs[:n_x], refs[n_x:n_x + 5]
    o_refs, act_ref = refs[n_x + 5:n_x + 5 + n_out], refs[-1]
    is_ctx = pl.program_id(0) < N_CTX // TM
    x = x_refs[0][...] if n_x == 1 else jnp.where(is_ctx, x_refs[0][...], x_refs[1][...])
    m = mod_ref[0]
    h = _modulate(x, gpre_ref[0], m).astype(BF16)
    for c in range(D_FF // FF_CW):
        g = _dot(h, wgu_ref[:, c * FF_CW:(c + 1) * FF_CW])
        u = _dot(h, wgu_ref[:, D_FF + c * FF_CW:D_FF + (c + 1) * FF_CW])
        act_ref[:, c * FF_CW:(c + 1) * FF_CW] = (_silu(g) * u).astype(BF16)
    y = _dot(act_ref[...], wd_ref[...])
    out = _residual(x, y, gpost_ref[0], m, 0.5)
    if n_out == 1:
        o_refs[0][...] = out
    else:
        @pl.when(is_ctx)
        def _():
            o_refs[0][...] = out

        @pl.when(jnp.logical_not(is_ctx))
        def _():
            o_refs[1][...] = out


def _ffn(xs, mods, ng, wgu, wd, layer, j, split_out=False):
    s = 2 * j
    nct = N_CTX // TM
    ctx_spec = pl.BlockSpec((TM, D), lambda i: (jnp.minimum(i, nct - 1), 0))
    lat_spec = pl.BlockSpec((TM, D), lambda i: (jnp.maximum(i - nct, 0), 0))
    all_spec = pl.BlockSpec((TM, D), lambda i: (i, 0))
    if split_out:
        out_shape = [jax.ShapeDtypeStruct((N_CTX, D), F32), jax.ShapeDtypeStruct((N_LAT, D), F32)]
        out_specs = [ctx_spec, lat_spec]
    else:
        out_shape, out_specs = jax.ShapeDtypeStruct((N_TOK, D), F32), all_spec
    return pl.pallas_call(
        functools.partial(_ffn_kernel, n_x=len(xs), n_out=2 if split_out else 1),
        out_shape=out_shape,
        grid=(N_TOK // TM,),
        in_specs=([all_spec] if len(xs) == 1 else [ctx_spec, lat_spec])
                 + [pl.BlockSpec((1, 1, 3 * D), lambda i: (layer * 3 + _seg_of_tile(i, TM), 0, s)),
                    pl.BlockSpec((1, 1, D), lambda i: (layer * 6 + 2 * s, 0, 0)),
                    pl.BlockSpec((1, 1, D), lambda i: (layer * 6 + 2 * s + 1, 0, 0)),
                    _resident((None, None, D, 2 * D_FF), lambda i: (layer, j, 0, 0)),
                    _resident((None, None, D_FF, D), lambda i: (layer, j, 0, 0))],
        out_specs=out_specs,
        scratch_shapes=[pltpu.VMEM((TM, D_FF), BF16)],
        compiler_params=_cparams(("arbitrary",)),
        name="ffn",
    )(*xs, mods, ng, ng, wgu, wd)


def _proj_kernel(x_ref, mod_ref, g_ref, w_ref, o_ref):
    h = _modulate(x_ref[...], g_ref[0], mod_ref[0]).astype(BF16)
    o_ref[...] = _dot(h, w_ref[...])


def _proj(x, mods, ng, w, layer):
    n = w.shape[1]
    return pl.pallas_call(
        _proj_kernel,
        out_shape=jax.ShapeDtypeStruct((N_TOK, n), F32),
        grid=(N_TOK // TM,),
        in_specs=[pl.BlockSpec((TM, D), lambda i: (i, 0)),
                  pl.BlockSpec((1, 1, 3 * D), lambda i: (layer * 3 + _seg_of_tile(i, TM), 0, 1)),
                  pl.BlockSpec((1, 1, D), lambda i: (layer * 6 + 2, 0, 0)),
                  _resident((D, n), lambda i: (0, 0))],
        out_specs=pl.BlockSpec((TM, n), lambda i: (i, 0)),
        compiler_params=_cparams(("parallel",)),
        name="proj",
    )(x, mods, ng, w)


def _out_pair_kernel(actx_ref, alat_ref, w_ref, x_ref, mod_ref, g_ref, o_ref):
    i = pl.program_id(0)
    a = jnp.where(i < N_CTX // TM, actx_ref[...], alat_ref[...])
    y = _dot(a.astype(BF16), w_ref[...])
    o_ref[...] = _residual(x_ref[...], y, g_ref[0], mod_ref[0], 1.0)


def _out_s5_kernel(yf_ref, yb_ref, w_ref, x_ref, mod_ref, g_ref, o_ref):
    a = jax.nn.gelu(yf_ref[...] + yb_ref[...]).astype(BF16)
    t = _dot(a, w_ref[...])
    y = t[:, :D] * jax.nn.sigmoid(t[:, D:])
    o_ref[...] = _residual(x_ref[...], y, g_ref[0], mod_ref[0], 1.0)


def _out_dn_kernel(of_ref, ob_ref, z_ref, og_ref, w_ref, x_ref, mod_ref, g_ref, o_ref, a_ref):
    o = of_ref[...] + ob_ref[...]
    og = og_ref[...]
    for h in range(DN_V):
        cs = slice(h * DN_HD, (h + 1) * DN_HD)
        oh = o[:, cs]
        on = oh * lax.rsqrt(jnp.mean(oh * oh, axis=-1, keepdims=True) + EPS) * og
        a_ref[:, cs] = (on * _silu(z_ref[:, cs])).astype(BF16)
    y = _dot(a_ref[...], w_ref[...])
    o_ref[...] = _residual(x_ref[...], y, g_ref[0], mod_ref[0], 1.0)


def _tail_specs(layer):
    return [pl.BlockSpec((TM, D), lambda i: (i, 0)),
            pl.BlockSpec((1, 1, 3 * D), lambda i: (layer * 3 + _seg_of_tile(i, TM), 0, 1)),
            pl.BlockSpec((1, 1, D), lambda i: (layer * 6 + 3, 0, 0))]


def _out_pair(actx, alat, w, x, mods, ng, layer):
    nct = N_CTX // TM
    return pl.pallas_call(
        _out_pair_kernel,
        out_shape=jax.ShapeDtypeStruct((N_TOK, D), F32),
        grid=(N_TOK // TM,),
        in_specs=[pl.BlockSpec((TM, D), lambda i: (jnp.minimum(i, nct - 1), 0)),
                  pl.BlockSpec((TM, D), lambda i: (jnp.maximum(i - nct, 0), 0)),
                  _resident((D, D), lambda i: (0, 0))] + _tail_specs(layer),
        out_specs=pl.BlockSpec((TM, D), lambda i: (i, 0)),
        compiler_params=_cparams(("parallel",)),
        name="out_pair",
    )(actx, alat, w, x, mods, ng)


def _out_s5(yf, yb, w, x, mods, ng, layer):
    return pl.pallas_call(
        _out_s5_kernel,
        out_shape=jax.ShapeDtypeStruct((N_TOK, D), F32),
        grid=(N_TOK // TM,),
        in_specs=[pl.BlockSpec((TM, D), lambda i: (i, 0)),
                  pl.BlockSpec((TM, D), lambda i: (i, 0)),
                  _resident((D, 2 * D), lambda i: (0, 0))] + _tail_specs(layer),
        out_specs=pl.BlockSpec((TM, D), lambda i: (i, 0)),
        compiler_params=_cparams(("parallel",)),
        name="out_s5",
    )(yf, yb, w, x, mods, ng)


def _out_dn(of, ob, proj, og, w, x, mods, ng, layer):
    zblk = 2 * DN_NQK + DN_NV
    return pl.pallas_call(
        _out_dn_kernel,
        out_shape=jax.ShapeDtypeStruct((N_TOK, D), F32),
        grid=(N_TOK // TM,),
        in_specs=[pl.BlockSpec((TM, D), lambda i: (i, 0)),
                  pl.BlockSpec((TM, D), lambda i: (i, 0)),
                  pl.BlockSpec((TM, DN_NV), lambda i: (i, zblk // DN_NV)),
                  pl.BlockSpec((1, DN_HD), lambda i: (0, 0)),
                  _resident((DN_NV, D), lambda i: (0, 0))] + _tail_specs(layer),
        out_specs=pl.BlockSpec((TM, D), lambda i: (i, 0)),
        scratch_shapes=[pltpu.VMEM((TM, DN_NV), BF16)],
        compiler_params=_cparams(("parallel",)),
        name="out_dn",
    )(of, ob, proj, og, w, x, mods, ng)


def _softmax_pv(parts, sink):
    m = functools.reduce(jnp.maximum, [jnp.max(s, axis=-1, keepdims=True) for s, _ in parts])
    if sink is not None:
        m = jnp.maximum(m, sink)
    l = None
    o = None
    for s, v in parts:
        p = jnp.exp(s - m)
        ls = jnp.sum(p, axis=-1, keepdims=True)
        os_ = _dot(p.astype(BF16), v)
        l = ls if l is None else l + ls
        o = os_ if o is None else o + os_
    if sink is not None:
        l = l + jnp.exp(sink - m)
    return o / l


def _attn_ctx_kernel(*refs, n_heads, group, has_sink):
    if has_sink:
        sink_ref, q_ref, k_ref, v_ref, o_ref = refs
    else:
        q_ref, k_ref, v_ref, o_ref = refs
    scale = HEAD ** -0.5
    for h in range(n_heads):
        kv = h // group
        q = (q_ref[:, h * HEAD:(h + 1) * HEAD] * scale).astype(BF16)
        k = k_ref[:, kv * HEAD:(kv + 1) * HEAD].astype(BF16)
        v = v_ref[:, kv * HEAD:(kv + 1) * HEAD].astype(BF16)
        s = _dot_nt(q, k)
        o = _softmax_pv([(s, v)], sink_ref[h] if has_sink else None)
        o_ref[:, h * HEAD:(h + 1) * HEAD] = o


def _attn_ctx(qkv, n_heads, n_kv, sink=None):
    qw, kw = n_heads * HEAD, n_kv * HEAD
    kern = functools.partial(_attn_ctx_kernel, n_heads=n_heads, group=n_heads // n_kv,
                             has_sink=sink is not None)
    specs = [pl.BlockSpec((CTX_L, qw), lambda b: (b, 0)),
             pl.BlockSpec((CTX_L, kw), lambda b: (b, qw // kw)),
             pl.BlockSpec((CTX_L, kw), lambda b: (b, qw // kw + 1))]
    args = [qkv, qkv, qkv]
    if sink is not None:
        specs = [pl.BlockSpec(memory_space=pltpu.SMEM)] + specs
        args = [sink] + args
    return pl.pallas_call(
        kern,
        out_shape=jax.ShapeDtypeStruct((N_CTX, qw), F32),
        grid=(CTX_B,),
        in_specs=specs,
        out_specs=pl.BlockSpec((CTX_L, qw), lambda b: (b, 0)),
        compiler_params=_cparams(("parallel",)),
        name="attn_ctx",
    )(*args)


def _rope_tables():
    n = HEAD // 4
    inv = ROPE_BASE ** (-jnp.arange(n, dtype=F32) / n)
    t = jnp.arange(LAT_L)
    ang_r = (t // GRID_W).astype(F32)[:, None] * inv[None, :]
    ang_c = (t % GRID_W).astype(F32)[:, None] * inv[None, :]
    cos = jnp.concatenate([jnp.cos(ang_r), jnp.cos(ang_r), jnp.cos(ang_c), jnp.cos(ang_c)], axis=-1)
    sin = jnp.concatenate([-jnp.sin(ang_r), jnp.sin(ang_r), -jnp.sin(ang_c), jnp.sin(ang_c)], axis=-1)
    return jnp.tile(cos, (1, 2)), jnp.tile(sin, (1, 2))


def _rope(x, cos, sin):
    rows = x.shape[0]
    lane = lax.broadcasted_iota(jnp.int32, (rows, 128), 1)
    first = (lane % 32) < 16
    outs = []
    for cb in range(x.shape[1] // 128):
        xb = x[:, cb * 128:(cb + 1) * 128]
        partner = jnp.where(first, pltpu.roll(xb, 112, 1), pltpu.roll(xb, 16, 1))
        outs.append(xb * cos + partner * sin)
    return jnp.concatenate(outs, axis=1)


A_KWIN = 3 * A_WIN


def _attn_a_lat_kernel(sink_ref, q_ref, k_ref, v_ref, kc_ref, vc_ref, cos_ref, sin_ref, o_ref):
    n = pl.program_id(1)
    nb = LAT_L // A_WIN
    start = pl.multiple_of(jnp.clip(n - 1, 0, nb - 3) * A_WIN, A_WIN)
    q0 = pl.multiple_of(n * A_WIN, A_WIN)
    scale = HEAD ** -0.5
    q = _rope(q_ref[...], cos_ref[pl.ds(q0, A_WIN), :], sin_ref[pl.ds(q0, A_WIN), :]) * scale
    k = _rope(k_ref[pl.ds(start, A_KWIN), :], cos_ref[pl.ds(start, A_KWIN), :],
              sin_ref[pl.ds(start, A_KWIN), :]).astype(BF16)
    v = v_ref[pl.ds(start, A_KWIN), :].astype(BF16)
    kc = kc_ref[0].astype(BF16)
    vc = vc_ref[0].astype(BF16)
    qpos = q0 + lax.broadcasted_iota(jnp.int32, (A_WIN, A_KWIN), 0)
    kpos = start + lax.broadcasted_iota(jnp.int32, (A_WIN, A_KWIN), 1)
    ok = jnp.abs(kpos - qpos) <= A_WIN
    group = A_HEADS // A_KV
    for h in range(A_HEADS):
        kvs = slice((h // group) * HEAD, (h // group + 1) * HEAD)
        qh = q[:, h * HEAD:(h + 1) * HEAD].astype(BF16)
        s_loc = jnp.where(ok, _dot_nt(qh, k[:, kvs]), NEG)
        s_ctx = _dot_nt(qh, kc[:, kvs])
        o = _softmax_pv([(s_loc, v[:, kvs]), (s_ctx, vc[:, kvs])], sink_ref[h])
        o_ref[:, h * HEAD:(h + 1) * HEAD] = o


def _attn_a_lat(qkv, sink, kc, vc, cos, sin):
    qw, kw = A_HEADS * HEAD, A_KV * HEAD
    nb = LAT_L // A_WIN
    return pl.pallas_call(
        _attn_a_lat_kernel,
        out_shape=jax.ShapeDtypeStruct((N_LAT, qw), F32),
        grid=(LAT_B, nb),
        in_specs=[pl.BlockSpec(memory_space=pltpu.SMEM),
                  pl.BlockSpec((A_WIN, qw), lambda b, n: (N_CTX // A_WIN + b * nb + n, 0)),
                  pl.BlockSpec((LAT_L, kw), lambda b, n: (N_CTX // LAT_L + b, qw // kw)),
                  pl.BlockSpec((LAT_L, kw), lambda b, n: (N_CTX // LAT_L + b, qw // kw + 1)),
                  pl.BlockSpec((1, PAST, kw), lambda b, n: (b, 0, 0)),
                  pl.BlockSpec((1, PAST, kw), lambda b, n: (b, 0, 0)),
                  _resident((LAT_L, 128), lambda b, n: (0, 0)),
                  _resident((LAT_L, 128), lambda b, n: (0, 0))],
        out_specs=pl.BlockSpec((A_WIN, qw), lambda b, n: (b * nb + n, 0)),
        compiler_params=_cparams(("parallel", "parallel")),
        name="attn_a_lat",
    )(sink, qkv, qkv, qkv, kc, vc, cos, sin)


NA_QROWS = 8
NA_QTOK = NA_QROWS * GRID_W
NA_KROWS = NA_QROWS + NA_ROWS
NA_KTOK = NA_KROWS * GRID_W
NA_KBLK = (NA_ROWS // 2) * GRID_W
NA_NKB = NA_KTOK // NA_KBLK
NA_HPAIR = 2


def _na_tables(rpb):
    col = jnp.arange(GRID_W)
    dcol = jnp.clip(col[None, :] - col[:, None], 1 - NA_COLS, NA_COLS - 1) + NA_COLS - 1
    cs = jnp.clip(col - NA_COLS // 2, 0, GRID_W - NA_COLS)
    okc = (col[None, :] >= cs[:, None]) & (col[None, :] < cs[:, None] + NA_COLS)
    t = jnp.where(okc[None, None], rpb[:, :, dcol], NEG)
    a = jnp.arange(NA_QROWS)[:, None]
    j = jnp.arange(NA_KROWS)[None, :]
    d = j - NA_ROWS // 2 - a + NA_ROWS - 1
    dv = (d >= 0) & (d < 2 * NA_ROWS - 1)
    b = jnp.where(dv[None, :, :, None, None], t[:, jnp.clip(d, 0, 2 * NA_ROWS - 2)], NEG)
    bias = jnp.transpose(b, (0, 1, 3, 2, 4)).reshape(C_HEADS, NA_QTOK, NA_KTOK)
    rows = LAT_L // GRID_W
    masks = []
    for blk in (0, 1, rows // NA_QROWS - 1):
        r = blk * NA_QROWS + a
        key_row = blk * NA_QROWS - NA_ROWS // 2 + j
        rs = jnp.clip(r - NA_ROWS // 2, 0, rows - NA_ROWS)
        valid = (key_row >= rs) & (key_row < rs + NA_ROWS)
        m = jnp.where(valid, 0.0, NEG).astype(F32)
        masks.append(jnp.broadcast_to(m[:, None, :, None], (NA_QROWS, GRID_W, NA_KROWS, GRID_W))
                     .reshape(NA_QTOK, NA_KTOK))
    return bias, jnp.stack(masks)


def _attn_c_lat_kernel(*refs):
    q_ref = refs[0]
    k_refs = refs[1:1 + NA_NKB]
    v_refs = refs[1 + NA_NKB:1 + 2 * NA_NKB]
    kc_ref, vc_ref, bias_ref, wm_ref, o_ref = refs[1 + 2 * NA_NKB:]
    i = pl.program_id(2)
    nblk = LAT_L // NA_QTOK
    var = jnp.where(i == 0, 0, jnp.where(i == nblk - 1, 2, 1))
    scale = HEAD ** -0.5
    for e in range(NA_HPAIR):
        hs = slice(e * HEAD, (e + 1) * HEAD)
        qh = (q_ref[:, hs] * scale).astype(BF16)
        parts = []
        for m in range(NA_NKB):
            cs = slice(m * NA_KBLK, (m + 1) * NA_KBLK)
            s = _dot_nt(qh, k_refs[m][:, hs]) + bias_ref[e, :, cs] + wm_ref[var, :, cs]
            parts.append((s, v_refs[m][:, hs]))
        parts.append((_dot_nt(qh, kc_ref[0, :, hs].astype(BF16)), vc_ref[0, :, hs].astype(BF16)))
        o_ref[:, hs] = _softmax_pv(parts, None)


def _attn_c_lat(qkv, kv_bf, kc, vc, bias, wmask):
    qw = C_HEADS * HEAD
    hw = NA_HPAIR * HEAD
    nblk = LAT_L // NA_QTOK
    nkb = LAT_L // NA_KBLK
    per = NA_QTOK // NA_KBLK

    def kv_spec(m, col0):
        def idx(hp, b, i):
            return (b * nkb + jnp.clip(i * per - 1 + m, 0, nkb - 1), col0 + hp)
        return pl.BlockSpec((NA_KBLK, hw), idx)

    return pl.pallas_call(
        _attn_c_lat_kernel,
        out_shape=jax.ShapeDtypeStruct((N_LAT, qw), F32),
        grid=(C_HEADS // NA_HPAIR, LAT_B, nblk),
        in_specs=[pl.BlockSpec((NA_QTOK, hw), lambda hp, b, i: (N_CTX // NA_QTOK + b * nblk + i, hp))]
                 + [kv_spec(m, 0) for m in range(NA_NKB)]
                 + [kv_spec(m, qw // hw) for m in range(NA_NKB)]
                 + [pl.BlockSpec((1, PAST, hw), lambda hp, b, i: (b, 0, hp)),
                    pl.BlockSpec((1, PAST, hw), lambda hp, b, i: (b, 0, hp)),
                    pl.BlockSpec((NA_HPAIR, NA_QTOK, NA_KTOK), lambda hp, b, i: (hp, 0, 0)),
                    _resident((3, NA_QTOK, NA_KTOK), lambda hp, b, i: (0, 0, 0))],
        out_specs=pl.BlockSpec((NA_QTOK, hw), lambda hp, b, i: (b * nblk + i, hp)),
        compiler_params=_cparams(("parallel", "parallel", "arbitrary")),
        name="attn_c_lat",
    )(qkv, *([kv_bf] * (2 * NA_NKB)), kc, vc, bias, wmask)


def _seq_tables():
    fb, bb, first, last, sid = [], [], [], [], []
    base = 0
    for s, length in enumerate([CTX_L] * CTX_B + [LAT_L] * LAT_B):
        n = length // SEQ_BLK
        for c in range(n):
            fb.append(base + c)
            bb.append(base + n - 1 - c)
            first.append(int(c == 0))
            last.append(int(c == n - 1))
            sid.append(s)
        base += n
    return tuple(jnp.asarray(np.array(t, np.int32)) for t in (fb, bb, first, last, sid))


def _seg_of_sid(s):
    return jnp.maximum(s - (CTX_B - 1), 0)


S5_GB = 8
S5_LB = 1024
S5_TILES = SEQ_BLK // 8


def _s5_tables(lam_re, lam_im, log_dt, b_re, b_im, c_re, c_im):
    dt = jnp.exp(log_dt)[..., None]
    lr, li = lam_re * dt, lam_im * dt
    a_re, a_im = jnp.exp(lr) * jnp.cos(li), jnp.exp(lr) * jnp.sin(li)
    den = lam_re * lam_re + lam_im * lam_im
    fr = ((a_re - 1.0) * lam_re + a_im * lam_im) / den
    fi = (a_im * lam_re - (a_re - 1.0) * lam_im) / den
    bb_re = fr[..., None] * b_re - fi[..., None] * b_im
    bb_im = fr[..., None] * b_im + fi[..., None] * b_re
    eye = jnp.eye(S5_GB, dtype=F32)

    def bdiag_in(t):
        t = t.reshape(2, S5_GROUPS // S5_GB, S5_GB, S5_STATE, S5_GROUP)
        return jnp.einsum('dbgpc,gh->dbgchp', t, eye).reshape(2, S5_GROUPS // S5_GB, 128, 512)

    def bdiag_out(t):
        t = t.reshape(2, S5_GROUPS // S5_GB, S5_GB, S5_GROUP, S5_STATE)
        return jnp.einsum('dbgcp,gh->dbgphc', t, eye).reshape(2, S5_GROUPS // S5_GB, 512, 128)

    wb = jnp.stack([bdiag_in(bb_re), bdiag_in(bb_im)], axis=1).astype(BF16)
    wc = jnp.stack([bdiag_out(c_re), bdiag_out(c_im)], axis=1).astype(BF16)

    ar, ai = a_re.reshape(2, S5_N), a_im.reshape(2, S5_N)
    pows = [(ar, ai)]
    for _ in range(7):
        pr, pi = pows[-1]
        pows.append((pr * ar - pi * ai, pr * ai + pi * ar))
    row = jnp.arange(8)[:, None]
    zero = jnp.zeros((8, S5_N), F32)
    ta, pt = [], []
    for d in range(2):
        steps = []
        for s in (1, 2, 4):
            keep = (row >= s) if d == 0 else (row + s <= 7)
            steps.append(jnp.stack([jnp.where(keep, pows[s - 1][ri][d][None, :], zero) for ri in range(2)]))
        ta.append(jnp.stack(steps))
        order = list(range(8)) if d == 0 else list(range(7, -1, -1))
        pt.append(jnp.stack([jnp.stack([pows[k][ri][d] for k in order]) for ri in range(2)]))
    return wb, wc, jnp.stack(ta), jnp.stack(pt)


def _s5_kernel(fb_ref, bb_ref, first_ref, last_ref, sid_ref,
               xf_ref, xb_ref, mod_ref, g_ref, dsk_ref, wb_ref, wc_ref, ta_ref, pt_ref, sre_ref, sim_ref,
               yf_ref, yb_ref, ore_ref, oim_ref, bur, bui, car):
    i = pl.program_id(0)
    is_first = first_ref[i] == 1
    is_lat = sid_ref[i] >= CTX_B

    @pl.when(is_first & is_lat)
    def _():
        car[:, 0] = sre_ref[0]
        car[:, 1] = sim_ref[0]

    @pl.when(is_first & jnp.logical_not(is_lat))
    def _():
        car[...] = jnp.zeros_like(car)

    for d in range(2):
        x_ref, y_ref = (xf_ref, yf_ref) if d == 0 else (xb_ref, yb_ref)
        hm = _modulate(x_ref[...], g_ref[0], mod_ref[0])
        hb = hm.astype(BF16)
        for gb in range(S5_GROUPS // S5_GB):
            ub = hb[:, gb * 128:(gb + 1) * 128]
            bur[:, gb * 512:(gb + 1) * 512] = _dot(ub, wb_ref[d, 0, gb])
            bui[:, gb * 512:(gb + 1) * 512] = _dot(ub, wb_ref[d, 1, gb])

        last_row = 7 if d == 0 else 0
        for lb in range(S5_N // S5_LB):
            cs = slice(lb * S5_LB, (lb + 1) * S5_LB)

            def body(t, carry, d=d, cs=cs, last_row=last_row):
                cr, ci = carry
                tt = t if d == 0 else S5_TILES - 1 - t
                r0 = pl.multiple_of(tt * 8, 8)
                xr = bur[pl.ds(r0, 8), cs]
                xi = bui[pl.ds(r0, 8), cs]
                for si, s in enumerate((1, 2, 4)):
                    sh = s if d == 0 else 8 - s
                    sr = pltpu.roll(xr, sh, 0)
                    sm = pltpu.roll(xi, sh, 0)
                    ar = ta_ref[d, si, 0, :, cs]
                    ai = ta_ref[d, si, 1, :, cs]
                    xr, xi = xr + ar * sr - ai * sm, xi + ar * sm + ai * sr
                pr = pt_ref[d, 0, :, cs]
                pi = pt_ref[d, 1, :, cs]
                xr, xi = xr + pr * cr - pi * ci, xi + pr * ci + pi * cr
                bur[pl.ds(r0, 8), cs] = xr
                bui[pl.ds(r0, 8), cs] = xi
                return xr[last_row:last_row + 1], xi[last_row:last_row + 1]

            cr, ci = lax.fori_loop(0, S5_TILES, body, (car[d, 0, :, cs], car[d, 1, :, cs]))
            car[d, 0, :, cs] = cr
            car[d, 1, :, cs] = ci

        for gb in range(S5_GROUPS // S5_GB):
            xr = bur[:, gb * 512:(gb + 1) * 512].astype(BF16)
            xi = bui[:, gb * 512:(gb + 1) * 512].astype(BF16)
            y = _dot(xr, wc_ref[d, 0, gb]) - _dot(xi, wc_ref[d, 1, gb])
            if d == 0:
                y = y + dsk_ref[:, gb * 128:(gb + 1) * 128] * hm[:, gb * 128:(gb + 1) * 128]
            y_ref[:, gb * 128:(gb + 1) * 128] = y

    ore_ref[0] = car[:, 0]
    oim_ref[0] = car[:, 1]


def _s5(x, mods, ng, dskip, tabs, state_re, state_im, layer, tables):
    wb, wc, ta, pt = tabs
    n_steps = tables[0].shape[0]
    n_seq = CTX_B + LAT_B

    def mod_idx(i, fb, bb, first, last, sid):
        return (layer * 3 + _seg_of_sid(sid[i]), 0, 1)

    def st_idx(i, fb, bb, first, last, sid):
        return (jnp.maximum(sid[i] - CTX_B, 0), 0, 0, 0)

    def const(nd):
        return lambda i, *_: (0,) * nd

    grid_spec = pltpu.PrefetchScalarGridSpec(
        num_scalar_prefetch=5,
        grid=(n_steps,),
        in_specs=[pl.BlockSpec((SEQ_BLK, D), lambda i, fb, *_: (fb[i], 0)),
                  pl.BlockSpec((SEQ_BLK, D), lambda i, fb, bb, *_: (bb[i], 0)),
                  pl.BlockSpec((1, 1, 3 * D), mod_idx),
                  pl.BlockSpec((1, 1, D), lambda i, *_: (layer * 6 + 2, 0, 0)),
                  pl.BlockSpec((1, D), const(2)),
                  _resident(wb.shape, const(5)),
                  _resident(wc.shape, const(5)),
                  _resident(ta.shape, const(5)),
                  _resident(pt.shape, const(4)),
                  pl.BlockSpec((1, 2, 1, S5_N), st_idx),
                  pl.BlockSpec((1, 2, 1, S5_N), st_idx)],
        out_specs=[pl.BlockSpec((SEQ_BLK, D), lambda i, fb, *_: (fb[i], 0)),
                   pl.BlockSpec((SEQ_BLK, D), lambda i, fb, bb, *_: (bb[i], 0)),
                   pl.BlockSpec((1, 2, 1, S5_N), lambda i, fb, bb, first, last, sid: (sid[i], 0, 0, 0)),
                   pl.BlockSpec((1, 2, 1, S5_N), lambda i, fb, bb, first, last, sid: (sid[i], 0, 0, 0))],
        scratch_shapes=[pltpu.VMEM((SEQ_BLK, S5_N), F32), pltpu.VMEM((SEQ_BLK, S5_N), F32),
                        pltpu.VMEM((2, 2, 1, S5_N), F32)])
    return pl.pallas_call(
        _s5_kernel,
        out_shape=[jax.ShapeDtypeStruct((N_TOK, D), F32), jax.ShapeDtypeStruct((N_TOK, D), F32),
                   jax.ShapeDtypeStruct((n_seq, 2, 1, S5_N), F32), jax.ShapeDtypeStruct((n_seq, 2, 1, S5_N), F32)],
        grid_spec=grid_spec,
        compiler_params=_cparams(("arbitrary",)),
        name="s5",
    )(*tables, x, x, mods, ng, dskip, wb, wc, ta, pt, state_re, state_im)


def _dn_prep_kernel(blk_ref, first_ref, last_ref, x_ref, prev_ref, next_ref, ba_ref, cw_ref, alog_ref, dtb_ref,
                    q_ref, k_ref, v_ref, g_ref):
    i = pl.program_id(0)
    nqkv = 2 * DN_NQK + DN_NV
    x = x_ref[...]
    pv = jnp.where(first_ref[i] == 1, 0.0, prev_ref[...])
    nx = jnp.where(last_ref[i] == 1, 0.0, next_ref[...])
    ext = jnp.concatenate([pv, x, nx], axis=0)
    n_ext = SEQ_BLK + 16
    acc = None
    for j in range(DN_CONV):
        off = DN_CONV // 2 - j
        e = ext if off == 0 else pltpu.roll(ext, off % n_ext, 0)
        term = cw_ref[j:j + 1, :] * e[8:8 + SEQ_BLK]
        acc = term if acc is None else acc + term
    a = _silu(acc)
    for h in range(DN_QK):
        cs = slice(h * DN_HD, (h + 1) * DN_HD)
        qh = a[:, cs]
        q_ref[:, cs] = qh * lax.rsqrt(jnp.sum(qh * qh, axis=-1, keepdims=True) + EPS) * (DN_HD ** -0.5)
        kh = a[:, DN_NQK + h * DN_HD:DN_NQK + (h + 1) * DN_HD]
        k_ref[:, cs] = kh * lax.rsqrt(jnp.sum(kh * kh, axis=-1, keepdims=True) + EPS)
    v_ref[...] = a[:, 2 * DN_NQK:nqkv]

    ba = ba_ref[...]
    lane = lax.broadcasted_iota(jnp.int32, ba.shape, 1)
    row = lax.broadcasted_iota(jnp.int32, ba.shape, 0) % DN_CHUNK
    is_g = ((lane % 16) >= 8) & (lane < 32)
    z = ba + dtb_ref[...]
    softplus = jnp.maximum(z, 0.0) + jnp.log1p(jnp.exp(-jnp.abs(z)))
    val = jnp.where(is_g, -jnp.exp(alog_ref[...]) * softplus, jax.nn.sigmoid(ba))
    cf = val
    cr = val
    s = 1
    while s < DN_CHUNK:
        cf = cf + jnp.where(row >= s, pltpu.roll(cf, s, 0), 0.0)
        cr = cr + jnp.where(row < DN_CHUNK - s, pltpu.roll(cr, SEQ_BLK - s, 0), 0.0)
        s *= 2
    g_ref[...] = jnp.where(lane < 32, jnp.where(is_g, jnp.where(lane < 16, cf, cr), val), 0.0)


def _dn_prep(proj, conv_w, alog_row, dtb_row, tables):
    fb, _, first, last, _ = tables
    nqkv = 2 * DN_NQK + DN_NV
    per = SEQ_BLK // 8
    n8 = N_TOK // 8
    grid_spec = pltpu.PrefetchScalarGridSpec(
        num_scalar_prefetch=3,
        grid=(fb.shape[0],),
        in_specs=[pl.BlockSpec((SEQ_BLK, nqkv), lambda i, blk, *_: (blk[i], 0)),
                  pl.BlockSpec((8, nqkv), lambda i, blk, *_: (jnp.maximum(blk[i] * per - 1, 0), 0)),
                  pl.BlockSpec((8, nqkv), lambda i, blk, *_: (jnp.minimum(blk[i] * per + per, n8 - 1), 0)),
                  pl.BlockSpec((SEQ_BLK, 128), lambda i, blk, *_: (blk[i], (nqkv + DN_NV) // 128)),
                  pl.BlockSpec((DN_CONV, nqkv), lambda i, *_: (0, 0)),
                  pl.BlockSpec((1, 128), lambda i, *_: (0, 0)),
                  pl.BlockSpec((1, 128), lambda i, *_: (0, 0))],
        out_specs=[pl.BlockSpec((SEQ_BLK, DN_NQK), lambda i, blk, *_: (blk[i], 0)),
                   pl.BlockSpec((SEQ_BLK, DN_NQK), lambda i, blk, *_: (blk[i], 0)),
                   pl.BlockSpec((SEQ_BLK, DN_NV), lambda i, blk, *_: (blk[i], 0)),
                   pl.BlockSpec((SEQ_BLK, 128), lambda i, blk, *_: (blk[i], 0))])
    return pl.pallas_call(
        _dn_prep_kernel,
        out_shape=[jax.ShapeDtypeStruct((N_TOK, DN_NQK), F32), jax.ShapeDtypeStruct((N_TOK, DN_NQK), F32),
                   jax.ShapeDtypeStruct((N_TOK, DN_NV), F32), jax.ShapeDtypeStruct((N_TOK, 128), F32)],
        grid_spec=grid_spec,
        compiler_params=_cparams(("arbitrary",)),
        name="dn_prep",
    )(fb, first, last, proj, proj, proj, proj, conv_w, alog_row, dtb_row)


DN_NCH = SEQ_BLK // DN_CHUNK


DN_LEVELS = int(math.log2(DN_CHUNK))


def _dn_setup(d, lane_beta, q, k, kk, qk, v, gates):
    n = SEQ_BLK
    lane = lax.broadcasted_iota(jnp.int32, gates.shape, 1)
    beta = jnp.sum(jnp.where(lane == lane_beta, gates, 0.0), axis=1, keepdims=True)
    gcol = jnp.sum(jnp.where(lane == lane_beta + 8, gates, 0.0), axis=1, keepdims=True)
    ri = lax.broadcasted_iota(jnp.int32, (n, n), 0)
    ci = lax.broadcasted_iota(jnp.int32, (n, n), 1)
    eye = ri == ci
    same = (ri // DN_CHUNK) == (ci // DN_CHUNK)
    grow = jnp.sum(jnp.where(eye, gcol, 0.0), axis=0, keepdims=True)
    if d == 0:
        lower, strict = same & (ri >= ci), same & (ri > ci)
        last_of = (ri // DN_CHUNK) * DN_CHUNK + DN_CHUNK - 1
    else:
        lower, strict = same & (ri <= ci), same & (ri < ci)
        last_of = (ri // DN_CHUNK) * DN_CHUNK
    decay = jnp.exp(jnp.where(lower, gcol - grow, NEG))
    g_last = jnp.sum(jnp.where(ci == last_of, grow, 0.0), axis=1, keepdims=True)
    return dict(
        d=d,
        lmat=jnp.where(strict, beta * kk * decay, 0.0),
        rhs16=jnp.concatenate([v * beta, k * (beta * jnp.exp(gcol))], axis=1).astype(BF16),
        aqk16=jnp.where(lower, qk * decay, 0.0).astype(BF16),
        qe16=(q * jnp.exp(gcol)).astype(BF16),
        kd16=(k * jnp.exp(g_last - gcol)).astype(BF16),
        eg=jnp.exp(g_last))


def _dn_solve_all(chains):
    n = SEQ_BLK
    ri = lax.broadcasted_iota(jnp.int32, (n, n), 0)
    ci = lax.broadcasted_iota(jnp.int32, (n, n), 1)
    diff = ri ^ ci
    level = sum((diff >= (1 << b)).astype(jnp.int32) for b in range(DN_LEVELS + 1))
    ts = [jnp.where(ri == ci, 1.0, 0.0) - jnp.where(level == 1, c["lmat"], 0.0) for c in chains]
    for lv in range(2, DN_LEVELS + 1):
        t16 = [t.astype(BF16) for t in ts]
        xs = [_dot(jnp.where(level == lv, c["lmat"], 0.0).astype(BF16), t) for c, t in zip(chains, t16)]
        ts = [t - _dot(th, x.astype(BF16)) for t, th, x in zip(ts, t16, xs)]
    return [_dot(t.astype(BF16), c["rhs16"]) for c, t in zip(chains, ts)]


def _dn_scan_all(chains, sols, states):
    us = [s[:, :DN_HD] for s in sols]
    w16 = [s[:, DN_HD:].astype(BF16) for s in sols]
    outs = [[None] * DN_NCH for _ in chains]
    for step in range(DN_NCH):
        idx = [step if c["d"] == 0 else DN_NCH - 1 - step for c in chains]
        rows = [slice(i * DN_CHUNK, (i + 1) * DN_CHUNK) for i in idx]
        s16 = [s.astype(BF16) for s in states]
        vn16 = [(u[r] - _dot(w[r], s)).astype(BF16) for u, w, r, s in zip(us, w16, rows, s16)]
        for n_, (c, r, s, vn) in enumerate(zip(chains, rows, s16, vn16)):
            outs[n_][idx[n_]] = _dot(c["qe16"][r], s) + _dot(c["aqk16"][r, r], vn)
        states = [s * c["eg"][r.start:r.start + 1] + _dot_tn(c["kd16"][r], vn)
                  for s, c, r, vn in zip(states, chains, rows, vn16)]
    return [jnp.concatenate(o, axis=0) for o in outs], states


def _dn_kernel(fb_ref, bb_ref, first_ref, last_ref, sid_ref,
               qf_ref, kf_ref, vf_ref, gf_ref, qb_ref, kb_ref, vb_ref, gb_ref, s0_ref,
               of_ref, ob_ref, so_ref, s_ref):
    j = pl.program_id(0)
    i = pl.program_id(1)
    is_first = first_ref[i] == 1
    is_lat = sid_ref[i] >= CTX_B

    @pl.when(is_first & is_lat)
    def _():
        s_ref[...] = s0_ref[0]

    @pl.when(is_first & jnp.logical_not(is_lat))
    def _():
        s_ref[...] = jnp.zeros_like(s_ref)

    chains = []
    for d in range(2):
        q_ref, k_ref, v_ref, g_ref = (qf_ref, kf_ref, vf_ref, gf_ref) if d == 0 else (qb_ref, kb_ref, vb_ref, gb_ref)
        q, k, gates = q_ref[...], k_ref[...], g_ref[...]
        k16 = k.astype(BF16)
        kk = _dot_nt(k16, k16)
        qk = _dot_nt(q.astype(BF16), k16)
        for e in range(DN_REP):
            chains.append(_dn_setup(d, 16 * d + DN_REP * j + e, q, k, kk, qk,
                                    v_ref[:, e * DN_HD:(e + 1) * DN_HD], gates))
    sols = _dn_solve_all(chains)
    outs, states = _dn_scan_all(chains, sols, [s_ref[d, e] for d in range(2) for e in range(DN_REP)])
    for n_, (o, s_new) in enumerate(zip(outs, states)):
        d, e = divmod(n_, DN_REP)
        (of_ref if d == 0 else ob_ref)[:, e * DN_HD:(e + 1) * DN_HD] = o
        s_ref[d, e] = s_new
    so_ref[0] = s_ref[...]


def _dn(qn, kn, vv, gates, state, tables):
    n_steps = tables[0].shape[0]
    n_seq = CTX_B + LAT_B

    def fwd(col):
        return lambda j, i, fb, *_: (fb[i], col(j))

    def bwd(col):
        return lambda j, i, fb, bb, *_: (bb[i], col(j))

    head = lambda j: j
    gate = lambda j: 0

    def st_in(j, i, fb, bb, first, last, sid):
        return (jnp.maximum(sid[i] - CTX_B, 0), 0, j, 0, 0)

    def st_out(j, i, fb, bb, first, last, sid):
        return (sid[i], 0, j, 0, 0)

    blk = lambda w: (SEQ_BLK, w)
    vw = DN_REP * DN_HD
    grid_spec = pltpu.PrefetchScalarGridSpec(
        num_scalar_prefetch=5,
        grid=(DN_QK, n_steps),
        in_specs=[pl.BlockSpec(blk(DN_HD), fwd(head)), pl.BlockSpec(blk(DN_HD), fwd(head)),
                  pl.BlockSpec(blk(vw), fwd(head)), pl.BlockSpec(blk(128), fwd(gate)),
                  pl.BlockSpec(blk(DN_HD), bwd(head)), pl.BlockSpec(blk(DN_HD), bwd(head)),
                  pl.BlockSpec(blk(vw), bwd(head)), pl.BlockSpec(blk(128), bwd(gate)),
                  pl.BlockSpec((1, 2, DN_REP, DN_HD, DN_HD), st_in)],
        out_specs=[pl.BlockSpec(blk(vw), fwd(head)), pl.BlockSpec(blk(vw), bwd(head)),
                   pl.BlockSpec((1, 2, DN_REP, DN_HD, DN_HD), st_out)],
        scratch_shapes=[pltpu.VMEM((2, DN_REP, DN_HD, DN_HD), F32)])
    return pl.pallas_call(
        _dn_kernel,
        out_shape=[jax.ShapeDtypeStruct((N_TOK, DN_NV), F32), jax.ShapeDtypeStruct((N_TOK, DN_NV), F32),
                   jax.ShapeDtypeStruct((n_seq, 2, DN_V, DN_HD, DN_HD), F32)],
        grid_spec=grid_spec,
        compiler_params=_cparams(("arbitrary", "arbitrary")),
        name="dn",
    )(*tables, qn, kn, vv, gates, qn, kn, vv, gates, state)


def kernel(x_prompt, x_sample, cache_attn_k, cache_attn_v, state_s5_re, state_s5_im, cache_na_k, cache_na_v,
           state_dn, c, c_ctx, norm_g, w_ada, b_ada, ffn_w_gu, ffn_w_d, a_w_qkv, a_w_o, a_sink,
           s5_lam_re, s5_lam_im, s5_log_dt, s5_b_re, s5_b_im, s5_c_re, s5_c_im, s5_d, s5_w_glu,
           na_w_qkv, na_w_o, na_rpb, dn_w_in, dn_conv_w, dn_w_ba, dn_a_log, dn_dt_bias, dn_out_g, dn_w_o):
    depth = w_ada.shape[0]
    cond8 = jnp.concatenate([c_ctx[None, :], c, jnp.zeros((8 - 1 - LAT_B, D), F32)], axis=0)
    mods = _ada(cond8, w_ada, b_ada)[:, :1 + LAT_B].reshape(depth * (1 + LAT_B), 1, 9 * D)
    ng = norm_g.reshape(depth * 6, 1, D)
    wgu = ffn_w_gu.astype(BF16)
    wd = ffn_w_d.astype(BF16)
    tables = _seq_tables()

    x = _ffn([x_prompt.reshape(N_CTX, D), x_sample.reshape(N_LAT, D)], mods, ng, wgu, wd, 0, 0)
    qkv = _proj(x, mods, ng, a_w_qkv[0].astype(BF16), 0)
    qw, kw = A_HEADS * HEAD, A_KV * HEAD
    cos, sin = _rope_tables()
    o_ctx = _attn_ctx(qkv, A_HEADS, A_KV, a_sink[0])
    o_lat = _attn_a_lat(qkv, a_sink[0], cache_attn_k[:, 0].reshape(LAT_B, PAST, kw),
                        cache_attn_v[:, 0].reshape(LAT_B, PAST, kw), cos, sin)
    new_attn_k = qkv[:N_CTX, qw:qw + kw].reshape(CTX_B, 1, CTX_L, A_KV, HEAD)
    new_attn_v = qkv[:N_CTX, qw + kw:].reshape(CTX_B, 1, CTX_L, A_KV, HEAD)
    x = _out_pair(o_ctx, o_lat, a_w_o[0].astype(BF16), x, mods, ng, 0)
    x = _ffn([x], mods, ng, wgu, wd, 0, 1)

    x = _ffn([x], mods, ng, wgu, wd, 1, 0)
    tabs = _s5_tables(s5_lam_re[0], s5_lam_im[0], s5_log_dt[0], s5_b_re[0], s5_b_im[0], s5_c_re[0], s5_c_im[0])
    yf, yb, fin_re, fin_im = _s5(x, mods, ng, s5_d, tabs,
                                 state_s5_re[:, 0].reshape(LAT_B, 2, 1, S5_N),
                                 state_s5_im[:, 0].reshape(LAT_B, 2, 1, S5_N), 1, tables)
    new_s5_re = fin_re[:CTX_B].reshape(CTX_B, 1, 2, S5_GROUPS, S5_STATE)
    new_s5_im = fin_im[:CTX_B].reshape(CTX_B, 1, 2, S5_GROUPS, S5_STATE)
    x = _out_s5(yf, yb, s5_w_glu[0].astype(BF16), x, mods, ng, 1)
    x = _ffn([x], mods, ng, wgu, wd, 1, 1)

    x = _ffn([x], mods, ng, wgu, wd, 2, 0)
    qkv = _proj(x, mods, ng, na_w_qkv[0].astype(BF16), 2)
    cw = C_HEADS * HEAD
    o_ctx = _attn_ctx(qkv, C_HEADS, C_HEADS)
    o_lat = _attn_c_lat(qkv, qkv[N_CTX:, cw:].astype(BF16), cache_na_k[:, 0].reshape(LAT_B, PAST, cw),
                        cache_na_v[:, 0].reshape(LAT_B, PAST, cw), *_na_tables(na_rpb[0]))
    new_na_k = qkv[:N_CTX, cw:2 * cw].reshape(CTX_B, 1, CTX_L, C_HEADS, HEAD)
    new_na_v = qkv[:N_CTX, 2 * cw:].reshape(CTX_B, 1, CTX_L, C_HEADS, HEAD)
    x = _out_pair(o_ctx, o_lat, na_w_o[0].astype(BF16), x, mods, ng, 2)
    x = _ffn([x], mods, ng, wgu, wd, 2, 1)

    x = _ffn([x], mods, ng, wgu, wd, 3, 0)
    w_all = jnp.concatenate([dn_w_in[0], dn_w_ba[0, 0], dn_w_ba[0, 1],
                             jnp.zeros((D, 128 - 4 * DN_V), F32)], axis=1).astype(BF16)
    proj = _proj(x, mods, ng, w_all, 3)
    pad8 = jnp.zeros((DN_V,), F32)
    gate_row = lambda t: jnp.concatenate([pad8, t[0], pad8, t[1], jnp.zeros((128 - 4 * DN_V,), F32)])[None, :]
    qn, kn, vv, gates = _dn_prep(proj, dn_conv_w[0], gate_row(dn_a_log[0]), gate_row(dn_dt_bias[0]), tables)
    of, ob, fin_dn = _dn(qn, kn, vv, gates, state_dn[:, 0], tables)
    new_dn = fin_dn[:CTX_B][:, None]
    x = _out_dn(of, ob, proj, dn_out_g, dn_w_o[0].astype(BF16), x, mods, ng, 3)
    y, z = _ffn([x], mods, ng, wgu, wd, 3, 1, split_out=True)
    y = y.reshape(CTX_B, CTX_L, D)
    z = z.reshape(LAT_B, LAT_L, D)
    return (y, z, new_attn_k, new_attn_v, new_s5_re, new_s5_im, new_na_k, new_na_v, new_dn)
```

```python
import functools
import math

import numpy as np
import jax
import jax.numpy as jnp
from jax import lax
from jax.experimental import pallas as pl
from jax.experimental.pallas import tpu as pltpu

F32 = jnp.float32
BF16 = jnp.bfloat16

D = 1024
CTX_B, CTX_L = 16, 256
LAT_B, LAT_L = 2, 4096
N_CTX = CTX_B * CTX_L
N_LAT = LAT_B * LAT_L
N_TOK = N_CTX + N_LAT
PAST = 512
D_FF = 2816
EPS = 1e-6
NEG = -1e30
GRID_W = 64
HEAD = 64
A_HEADS, A_KV = 16, 4
A_WIN = 128
C_HEADS = 16
NA_ROWS, NA_COLS = 8, 16
ROPE_BASE = 10000.0
S5_GROUPS, S5_GROUP, S5_STATE = 64, 16, 64
S5_N = S5_GROUPS * S5_STATE
DN_QK, DN_V, DN_HD, DN_CONV, DN_CHUNK = 4, 8, 128, 5, 64
DN_REP = DN_V // DN_QK
DN_NQK = DN_QK * DN_HD
DN_NV = DN_V * DN_HD
DN_PROJ = 2 * DN_NQK + 2 * DN_NV + 128

TM = 512
SEQ_BLK = 256
VMEM_MB = 56
ADA_TN = 1152


def _cparams(sem, mb=VMEM_MB):
    return pltpu.CompilerParams(dimension_semantics=sem, vmem_limit_bytes=mb * 1024 * 1024)


def _resident(block, index_map):
    return pl.BlockSpec(block, index_map, pipeline_mode=pl.Buffered(1))


def _seg_of_tile(i, tm):
    nct = N_CTX // tm
    return jnp.where(i < nct, 0, 1 + (i - nct) // (LAT_L // tm))


def _dot(a, b):
    return jnp.dot(a, b, preferred_element_type=F32)


def _dot_nt(a, b):
    return lax.dot_general(a, b, (((1,), (1,)), ((), ())), preferred_element_type=F32)


def _dot_tn(a, b):
    return lax.dot_general(a, b, (((0,), (0,)), ((), ())), preferred_element_type=F32)


def _silu(x):
    return x * jax.nn.sigmoid(x)


def _modulate(x, g, m):
    xn = x * lax.rsqrt(jnp.mean(x * x, axis=-1, keepdims=True) + EPS) * g
    return xn * (1.0 + m[:, D:2 * D]) + m[:, :D]


def _residual(x, y, g, m, weight):
    yn = y * lax.rsqrt(jnp.mean(y * y, axis=-1, keepdims=True) + EPS) * g
    return x + weight * m[:, 2 * D:] * yn


def _ada_kernel(cond_ref, w_ref, b_ref, o_ref):
    s = _silu(cond_ref[...]).astype(BF16)
    o_ref[0] = _dot(s, w_ref[0].astype(BF16)) + b_ref[0]


def _ada(cond8, w_ada, b_ada):
    depth, _, n9 = w_ada.shape
    return pl.pallas_call(
        _ada_kernel,
        out_shape=jax.ShapeDtypeStruct((depth, 8, n9), F32),
        grid=(depth, n9 // ADA_TN),
        in_specs=[pl.BlockSpec((8, D), lambda l, j: (0, 0)),
                  pl.BlockSpec((1, D, ADA_TN), lambda l, j: (l, 0, j)),
                  pl.BlockSpec((1, 1, ADA_TN), lambda l, j: (l, 0, j))],
        out_specs=pl.BlockSpec((1, 8, ADA_TN), lambda l, j: (l, 0, j)),
        compiler_params=_cparams(("parallel", "parallel")),
        name="ada",
    )(cond8, w_ada, b_ada.reshape(depth, 1, n9))


FF_CW = 256


def _ffn_kernel(*refs, n_x, n_out):
    x_refs, (mod_ref, gpre_ref, gpost_ref, wgu_ref, wd_ref) = refs[:n_x], refs[n_x:n_x + 5]
    o_refs, act_ref = refs[n_x + 5:n_x + 5 + n_out], refs[-1]
    is_ctx = pl.program_id(0) < N_CTX // TM
    x = x_refs[0][...] if n_x == 1 else jnp.where(is_ctx, x_refs[0][...], x_refs[1][...])
    m = mod_ref[0]
    h = _modulate(x, gpre_ref[0], m).astype(BF16)
    for c in range(D_FF // FF_CW):
        g = _dot(h, wgu_ref[:, c * FF_CW:(c + 1) * FF_CW])
        u = _dot(h, wgu_ref[:, D_FF + c * FF_CW:D_FF + (c + 1) * FF_CW])
        act_ref[:, c * FF_CW:(c + 1) * FF_CW] = (_silu(g) * u).astype(BF16)
    y = _dot(act_ref[...], wd_ref[...])
    out = _residual(x, y, gpost_ref[0], m, 0.5)
    if n_out == 1:
        o_refs[0][...] = out
    else:
        @pl.when(is_ctx)
        def _():
            o_refs[0][...] = out

        @pl.when(jnp.logical_not(is_ctx))
        def _():
            o_refs[1][...] = out


def _ffn(xs, mods, ng, wgu, wd, layer, j, split_out=False):
    s = 2 * j
    nct = N_CTX // TM
    ctx_spec = pl.BlockSpec((TM, D), lambda i: (jnp.minimum(i, nct - 1), 0))
    lat_spec = pl.BlockSpec((TM, D), lambda i: (jnp.maximum(i - nct, 0), 0))
    all_spec = pl.BlockSpec((TM, D), lambda i: (i, 0))
    if split_out:
        out_shape = [jax.ShapeDtypeStruct((N_CTX, D), F32), jax.ShapeDtypeStruct((N_LAT, D), F32)]
        out_specs = [ctx_spec, lat_spec]
    else:
        out_shape, out_specs = jax.ShapeDtypeStruct((N_TOK, D), F32), all_spec
    return pl.pallas_call(
        functools.partial(_ffn_kernel, n_x=len(xs), n_out=2 if split_out else 1),
        out_shape=out_shape,
        grid=(N_TOK // TM,),
        in_specs=([all_spec] if len(xs) == 1 else [ctx_spec, lat_spec])
                 + [pl.BlockSpec((1, 1, 3 * D), lambda i: (layer * 3 + _seg_of_tile(i, TM), 0, s)),
                    pl.BlockSpec((1, 1, D), lambda i: (layer * 6 + 2 * s, 0, 0)),
                    pl.BlockSpec((1, 1, D), lambda i: (layer * 6 + 2 * s + 1, 0, 0)),
                    _resident((None, None, D, 2 * D_FF), lambda i: (layer, j, 0, 0)),
                    _resident((None, None, D_FF, D), lambda i: (layer, j, 0, 0))],
        out_specs=out_specs,
        scratch_shapes=[pltpu.VMEM((TM, D_FF), BF16)],
        compiler_params=_cparams(("arbitrary",)),
        name="ffn",
    )(*xs, mods, ng, ng, wgu, wd)


def _proj_kernel(x_ref, mod_ref, g_ref, w_ref, o_ref):
    h = _modulate(x_ref[...], g_ref[0], mod_ref[0]).astype(BF16)
    o_ref[...] = _dot(h, w_ref[...])


def _proj(x, mods, ng, w, layer):
    n = w.shape[1]
    return pl.pallas_call(
        _proj_kernel,
        out_shape=jax.ShapeDtypeStruct((N_TOK, n), F32),
        grid=(N_TOK // TM,),
        in_specs=[pl.BlockSpec((TM, D), lambda i: (i, 0)),
                  pl.BlockSpec((1, 1, 3 * D), lambda i: (layer * 3 + _seg_of_tile(i, TM), 0, 1)),
                  pl.BlockSpec((1, 1, D), lambda i: (layer * 6 + 2, 0, 0)),
                  _resident((D, n), lambda i: (0, 0))],
        out_specs=pl.BlockSpec((TM, n), lambda i: (i, 0)),
        compiler_params=_cparams(("parallel",)),
        name="proj",
    )(x, mods, ng, w)


def _out_pair_kernel(actx_ref, alat_ref, w_ref, x_ref, mod_ref, g_ref, o_ref):
    i = pl.program_id(0)
    a = jnp.where(i < N_CTX // TM, actx_ref[...], alat_ref[...])
    y = _dot(a.astype(BF16), w_ref[...])
    o_ref[...] = _residual(x_ref[...], y, g_ref[0], mod_ref[0], 1.0)


def _out_s5_kernel(yc_ref, yl_ref, dsk_ref, gpre_ref, w_ref, x_ref, mod_ref, g_ref, o_ref):
    i = pl.program_id(0)
    y = jnp.where(i < N_CTX // TM, yc_ref[...], yl_ref[...])
    y = y + dsk_ref[...] * _modulate(x_ref[...], gpre_ref[0], mod_ref[0])
    a = jax.nn.gelu(y).astype(BF16)
    t = _dot(a, w_ref[...])
    y = t[:, :D] * jax.nn.sigmoid(t[:, D:])
    o_ref[...] = _residual(x_ref[...], y, g_ref[0], mod_ref[0], 1.0)


def _out_dn_kernel(of_ref, ob_ref, z_ref, og_ref, w_ref, x_ref, mod_ref, g_ref, o_ref, a_ref):
    o = of_ref[...] + ob_ref[...]
    og = og_ref[...]
    for h in range(DN_V):
        cs = slice(h * DN_HD, (h + 1) * DN_HD)
        oh = o[:, cs]
        on = oh * lax.rsqrt(jnp.mean(oh * oh, axis=-1, keepdims=True) + EPS) * og
        a_ref[:, cs] = (on * _silu(z_ref[:, cs])).astype(BF16)
    y = _dot(a_ref[...], w_ref[...])
    o_ref[...] = _residual(x_ref[...], y, g_ref[0], mod_ref[0], 1.0)


def _tail_specs(layer):
    return [pl.BlockSpec((TM, D), lambda i: (i, 0)),
            pl.BlockSpec((1, 1, 3 * D), lambda i: (layer * 3 + _seg_of_tile(i, TM), 0, 1)),
            pl.BlockSpec((1, 1, D), lambda i: (layer * 6 + 3, 0, 0))]


def _out_pair(actx, alat, w, x, mods, ng, layer):
    nct = N_CTX // TM
    return pl.pallas_call(
        _out_pair_kernel,
        out_shape=jax.ShapeDtypeStruct((N_TOK, D), F32),
        grid=(N_TOK // TM,),
        in_specs=[pl.BlockSpec((TM, D), lambda i: (jnp.minimum(i, nct - 1), 0)),
                  pl.BlockSpec((TM, D), lambda i: (jnp.maximum(i - nct, 0), 0)),
                  _resident((D, D), lambda i: (0, 0))] + _tail_specs(layer),
        out_specs=pl.BlockSpec((TM, D), lambda i: (i, 0)),
        compiler_params=_cparams(("parallel",)),
        name="out_pair",
    )(actx, alat, w, x, mods, ng)


def _out_s5(y_ctx, y_lat, dskip, w, x, mods, ng, layer):
    nct = N_CTX // TM
    return pl.pallas_call(
        _out_s5_kernel,
        out_shape=jax.ShapeDtypeStruct((N_TOK, D), F32),
        grid=(N_TOK // TM,),
        in_specs=[pl.BlockSpec((TM, D), lambda i: (jnp.minimum(i, nct - 1), 0)),
                  pl.BlockSpec((TM, D), lambda i: (jnp.maximum(i - nct, 0), 0)),
                  pl.BlockSpec((1, D), lambda i: (0, 0)),
                  pl.BlockSpec((1, 1, D), lambda i: (layer * 6 + 2, 0, 0)),
                  _resident((D, 2 * D), lambda i: (0, 0))] + _tail_specs(layer),
        out_specs=pl.BlockSpec((TM, D), lambda i: (i, 0)),
        compiler_params=_cparams(("parallel",)),
        name="out_s5",
    )(y_ctx, y_lat, dskip, ng, w, x, mods, ng)


def _out_dn(of, ob, proj, og, w, x, mods, ng, layer):
    zblk = 2 * DN_NQK + DN_NV
    return pl.pallas_call(
        _out_dn_kernel,
        out_shape=jax.ShapeDtypeStruct((N_TOK, D), F32),
        grid=(N_TOK // TM,),
        in_specs=[pl.BlockSpec((TM, D), lambda i: (i, 0)),
                  pl.BlockSpec((TM, D), lambda i: (i, 0)),
                  pl.BlockSpec((TM, DN_NV), lambda i: (i, zblk // DN_NV)),
                  pl.BlockSpec((1, DN_HD), lambda i: (0, 0)),
                  _resident((DN_NV, D), lambda i: (0, 0))] + _tail_specs(layer),
        out_specs=pl.BlockSpec((TM, D), lambda i: (i, 0)),
        scratch_shapes=[pltpu.VMEM((TM, DN_NV), BF16)],
        compiler_params=_cparams(("parallel",)),
        name="out_dn",
    )(of, ob, proj, og, w, x, mods, ng)


def _softmax_pv(parts, sink):
    m = functools.reduce(jnp.maximum, [jnp.max(s, axis=-1, keepdims=True) for s, _ in parts])
    if sink is not None:
        m = jnp.maximum(m, sink)
    l = None
    o = None
    for s, v in parts:
        p = jnp.exp(s - m)
        ls = jnp.sum(p, axis=-1, keepdims=True)
        os_ = _dot(p.astype(BF16), v)
        l = ls if l is None else l + ls
        o = os_ if o is None else o + os_
    if sink is not None:
        l = l + jnp.exp(sink - m)
    return o / l


def _attn_ctx_kernel(*refs, n_heads, group, has_sink):
    if has_sink:
        sink_ref, q_ref, k_ref, v_ref, o_ref = refs
    else:
        q_ref, k_ref, v_ref, o_ref = refs
    scale = HEAD ** -0.5
    for h in range(n_heads):
        kv = h // group
        q = (q_ref[:, h * HEAD:(h + 1) * HEAD] * scale).astype(BF16)
        k = k_ref[:, kv * HEAD:(kv + 1) * HEAD].astype(BF16)
        v = v_ref[:, kv * HEAD:(kv + 1) * HEAD].astype(BF16)
        s = _dot_nt(q, k)
        o = _softmax_pv([(s, v)], sink_ref[h] if has_sink else None)
        o_ref[:, h * HEAD:(h + 1) * HEAD] = o


def _attn_ctx(qkv, n_heads, n_kv, sink=None):
    qw, kw = n_heads * HEAD, n_kv * HEAD
    kern = functools.partial(_attn_ctx_kernel, n_heads=n_heads, group=n_heads // n_kv,
                             has_sink=sink is not None)
    specs = [pl.BlockSpec((CTX_L, qw), lambda b: (b, 0)),
             pl.BlockSpec((CTX_L, kw), lambda b: (b, qw // kw)),
             pl.BlockSpec((CTX_L, kw), lambda b: (b, qw // kw + 1))]
    args = [qkv, qkv, qkv]
    if sink is not None:
        specs = [pl.BlockSpec(memory_space=pltpu.SMEM)] + specs
        args = [sink] + args
    return pl.pallas_call(
        kern,
        out_shape=jax.ShapeDtypeStruct((N_CTX, qw), F32),
        grid=(CTX_B,),
        in_specs=specs,
        out_specs=pl.BlockSpec((CTX_L, qw), lambda b: (b, 0)),
        compiler_params=_cparams(("parallel",)),
        name="attn_ctx",
    )(*args)


def _rope_tables():
    n = HEAD // 4
    inv = ROPE_BASE ** (-jnp.arange(n, dtype=F32) / n)
    t = jnp.arange(LAT_L)
    ang_r = (t // GRID_W).astype(F32)[:, None] * inv[None, :]
    ang_c = (t % GRID_W).astype(F32)[:, None] * inv[None, :]
    cos = jnp.concatenate([jnp.cos(ang_r), jnp.cos(ang_r), jnp.cos(ang_c), jnp.cos(ang_c)], axis=-1)
    sin = jnp.concatenate([-jnp.sin(ang_r), jnp.sin(ang_r), -jnp.sin(ang_c), jnp.sin(ang_c)], axis=-1)
    return jnp.tile(cos, (1, 2)), jnp.tile(sin, (1, 2))


def _rope(x, cos, sin):
    rows = x.shape[0]
    lane = lax.broadcasted_iota(jnp.int32, (rows, 128), 1)
    first = (lane % 32) < 16
    outs = []
    for cb in range(x.shape[1] // 128):
        xb = x[:, cb * 128:(cb + 1) * 128]
        partner = jnp.where(first, pltpu.roll(xb, 112, 1), pltpu.roll(xb, 16, 1))
        outs.append(xb * cos + partner * sin)
    return jnp.concatenate(outs, axis=1)


A_KWIN = 3 * A_WIN


def _attn_a_lat_kernel(sink_ref, q_ref, k_ref, v_ref, kc_ref, vc_ref, cos_ref, sin_ref, o_ref):
    n = pl.program_id(1)
    nb = LAT_L // A_WIN
    start = pl.multiple_of(jnp.clip(n - 1, 0, nb - 3) * A_WIN, A_WIN)
    q0 = pl.multiple_of(n * A_WIN, A_WIN)
    scale = HEAD ** -0.5
    q = _rope(q_ref[...], cos_ref[pl.ds(q0, A_WIN), :], sin_ref[pl.ds(q0, A_WIN), :]) * scale
    k = _rope(k_ref[pl.ds(start, A_KWIN), :], cos_ref[pl.ds(start, A_KWIN), :],
              sin_ref[pl.ds(start, A_KWIN), :]).astype(BF16)
    v = v_ref[pl.ds(start, A_KWIN), :].astype(BF16)
    kc = kc_ref[0].astype(BF16)
    vc = vc_ref[0].astype(BF16)
    qpos = q0 + lax.broadcasted_iota(jnp.int32, (A_WIN, A_KWIN), 0)
    kpos = start + lax.broadcasted_iota(jnp.int32, (A_WIN, A_KWIN), 1)
    ok = jnp.abs(kpos - qpos) <= A_WIN
    group = A_HEADS // A_KV
    for h in range(A_HEADS):
        kvs = slice((h // group) * HEAD, (h // group + 1) * HEAD)
        qh = q[:, h * HEAD:(h + 1) * HEAD].astype(BF16)
        s_loc = jnp.where(ok, _dot_nt(qh, k[:, kvs]), NEG)
        s_ctx = _dot_nt(qh, kc[:, kvs])
        o = _softmax_pv([(s_loc, v[:, kvs]), (s_ctx, vc[:, kvs])], sink_ref[h])
        o_ref[:, h * HEAD:(h + 1) * HEAD] = o


def _attn_a_lat(qkv, sink, kc, vc, cos, sin):
    qw, kw = A_HEADS * HEAD, A_KV * HEAD
    nb = LAT_L // A_WIN
    return pl.pallas_call(
        _attn_a_lat_kernel,
        out_shape=jax.ShapeDtypeStruct((N_LAT, qw), F32),
        grid=(LAT_B, nb),
        in_specs=[pl.BlockSpec(memory_space=pltpu.SMEM),
                  pl.BlockSpec((A_WIN, qw), lambda b, n: (N_CTX // A_WIN + b * nb + n, 0)),
                  pl.BlockSpec((LAT_L, kw), lambda b, n: (N_CTX // LAT_L + b, qw // kw)),
                  pl.BlockSpec((LAT_L, kw), lambda b, n: (N_CTX // LAT_L + b, qw // kw + 1)),
                  pl.BlockSpec((1, PAST, kw), lambda b, n: (b, 0, 0)),
                  pl.BlockSpec((1, PAST, kw), lambda b, n: (b, 0, 0)),
                  _resident((LAT_L, 128), lambda b, n: (0, 0)),
                  _resident((LAT_L, 128), lambda b, n: (0, 0))],
        out_specs=pl.BlockSpec((A_WIN, qw), lambda b, n: (b * nb + n, 0)),
        compiler_params=_cparams(("parallel", "parallel")),
        name="attn_a_lat",
    )(sink, qkv, qkv, qkv, kc, vc, cos, sin)


NA_QROWS = 8
NA_QTOK = NA_QROWS * GRID_W
NA_KROWS = NA_QROWS + NA_ROWS
NA_KTOK = NA_KROWS * GRID_W
NA_KBLK = (NA_ROWS // 2) * GRID_W
NA_NKB = NA_KTOK // NA_KBLK
NA_HPAIR = 2


def _na_tables(rpb):
    col = jnp.arange(GRID_W)
    dcol = jnp.clip(col[None, :] - col[:, None], 1 - NA_COLS, NA_COLS - 1) + NA_COLS - 1
    cs = jnp.clip(col - NA_COLS // 2, 0, GRID_W - NA_COLS)
    okc = (col[None, :] >= cs[:, None]) & (col[None, :] < cs[:, None] + NA_COLS)
    t = jnp.where(okc[None, None], rpb[:, :, dcol], NEG)
    a = jnp.arange(NA_QROWS)[:, None]
    j = jnp.arange(NA_KROWS)[None, :]
    d = j - NA_ROWS // 2 - a + NA_ROWS - 1
    dv = (d >= 0) & (d < 2 * NA_ROWS - 1)
    b = jnp.where(dv[None, :, :, None, None], t[:, jnp.clip(d, 0, 2 * NA_ROWS - 2)], NEG)
    bias = jnp.transpose(b, (0, 1, 3, 2, 4)).reshape(C_HEADS, NA_QTOK, NA_KTOK)
    rows = LAT_L // GRID_W
    masks = []
    for blk in (0, 1, rows // NA_QROWS - 1):
        r = blk * NA_QROWS + a
        key_row = blk * NA_QROWS - NA_ROWS // 2 + j
        rs = jnp.clip(r - NA_ROWS // 2, 0, rows - NA_ROWS)
        valid = (key_row >= rs) & (key_row < rs + NA_ROWS)
        m = jnp.where(valid, 0.0, NEG).astype(F32)
        masks.append(jnp.broadcast_to(m[:, None, :, None], (NA_QROWS, GRID_W, NA_KROWS, GRID_W))
                     .reshape(NA_QTOK, NA_KTOK))
    return bias, jnp.stack(masks)


def _attn_c_lat_kernel(*refs):
    q_ref = refs[0]
    k_refs = refs[1:1 + NA_NKB]
    v_refs = refs[1 + NA_NKB:1 + 2 * NA_NKB]
    kc_ref, vc_ref, bias_ref, wm_ref, o_ref = refs[1 + 2 * NA_NKB:]
    i = pl.program_id(2)
    nblk = LAT_L // NA_QTOK
    var = jnp.where(i == 0, 0, jnp.where(i == nblk - 1, 2, 1))
    scale = HEAD ** -0.5
    for e in range(NA_HPAIR):
        hs = slice(e * HEAD, (e + 1) * HEAD)
        qh = (q_ref[:, hs] * scale).astype(BF16)
        parts = []
        for m in range(NA_NKB):
            cs = slice(m * NA_KBLK, (m + 1) * NA_KBLK)
            s = _dot_nt(qh, k_refs[m][:, hs]) + bias_ref[e, :, cs] + wm_ref[var, :, cs]
            parts.append((s, v_refs[m][:, hs]))
        parts.append((_dot_nt(qh, kc_ref[0, :, hs].astype(BF16)), vc_ref[0, :, hs].astype(BF16)))
        o_ref[:, hs] = _softmax_pv(parts, None)


def _attn_c_lat(qkv, kv_bf, kc, vc, bias, wmask):
    qw = C_HEADS * HEAD
    hw = NA_HPAIR * HEAD
    nblk = LAT_L // NA_QTOK
    nkb = LAT_L // NA_KBLK
    per = NA_QTOK // NA_KBLK

    def kv_spec(m, col0):
        def idx(hp, b, i):
            return (b * nkb + jnp.clip(i * per - 1 + m, 0, nkb - 1), col0 + hp)
        return pl.BlockSpec((NA_KBLK, hw), idx)

    return pl.pallas_call(
        _attn_c_lat_kernel,
        out_shape=jax.ShapeDtypeStruct((N_LAT, qw), F32),
        grid=(C_HEADS // NA_HPAIR, LAT_B, nblk),
        in_specs=[pl.BlockSpec((NA_QTOK, hw), lambda hp, b, i: (N_CTX // NA_QTOK + b * nblk + i, hp))]
                 + [kv_spec(m, 0) for m in range(NA_NKB)]
                 + [kv_spec(m, qw // hw) for m in range(NA_NKB)]
                 + [pl.BlockSpec((1, PAST, hw), lambda hp, b, i: (b, 0, hp)),
                    pl.BlockSpec((1, PAST, hw), lambda hp, b, i: (b, 0, hp)),
                    pl.BlockSpec((NA_HPAIR, NA_QTOK, NA_KTOK), lambda hp, b, i: (hp, 0, 0)),
                    _resident((3, NA_QTOK, NA_KTOK), lambda hp, b, i: (0, 0, 0))],
        out_specs=pl.BlockSpec((NA_QTOK, hw), lambda hp, b, i: (b * nblk + i, hp)),
        compiler_params=_cparams(("parallel", "parallel", "arbitrary")),
        name="attn_c_lat",
    )(qkv, *([kv_bf] * (2 * NA_NKB)), kc, vc, bias, wmask)


def _seq_tables():
    fb, bb, first, last, sid = [], [], [], [], []
    base = 0
    for s, length in enumerate([CTX_L] * CTX_B + [LAT_L] * LAT_B):
        n = length // SEQ_BLK
        for c in range(n):
            fb.append(base + c)
            bb.append(base + n - 1 - c)
            first.append(int(c == 0))
            last.append(int(c == n - 1))
            sid.append(s)
        base += n
    return tuple(jnp.asarray(np.array(t, np.int32)) for t in (fb, bb, first, last, sid))


S5_GB = 8
S5_GW = S5_GB * S5_STATE
S5_ROWS = N_CTX
assert LAT_L == S5_ROWS
S5_LAT_SEGS = 8


def _s5_tables(lam_re, lam_im, log_dt, b_re, b_im, c_re, c_im):
    dt = jnp.exp(log_dt)[..., None]
    lr, li = lam_re * dt, lam_im * dt
    a_re, a_im = jnp.exp(lr) * jnp.cos(li), jnp.exp(lr) * jnp.sin(li)
    den = lam_re * lam_re + lam_im * lam_im
    fr = ((a_re - 1.0) * lam_re + a_im * lam_im) / den
    fi = (a_im * lam_re - (a_re - 1.0) * lam_im) / den
    bb_re = fr[..., None] * b_re - fi[..., None] * b_im
    bb_im = fr[..., None] * b_im + fi[..., None] * b_re
    eye = jnp.eye(S5_GB, dtype=F32)

    def bdiag_in(t):
        t = t.reshape(2, S5_GROUPS // S5_GB, S5_GB, S5_STATE, S5_GROUP)
        return jnp.einsum('dbgpc,gh->dbgchp', t, eye).reshape(2, S5_GROUPS // S5_GB, 128, 512)

    def bdiag_out(t):
        t = t.reshape(2, S5_GROUPS // S5_GB, S5_GB, S5_GROUP, S5_STATE)
        return jnp.einsum('dbgcp,gh->dbgphc', t, eye).reshape(2, S5_GROUPS // S5_GB, 512, 128)

    wb = jnp.stack([bdiag_in(bb_re), bdiag_in(bb_im)], axis=1).astype(BF16)
    wc = jnp.stack([bdiag_out(c_re), bdiag_out(c_im)], axis=1).astype(BF16)

    abar = jnp.stack([a_re.reshape(2, S5_N), a_im.reshape(2, S5_N)], axis=1).reshape(4, S5_N)
    return wb, wc, abar


def _hmod_kernel(x_ref, mod_ref, g_ref, o_ref):
    o_ref[...] = _modulate(x_ref[...], g_ref[0], mod_ref[0]).astype(BF16)


def _hmod(x, mods, ng, layer):
    return pl.pallas_call(
        _hmod_kernel,
        out_shape=jax.ShapeDtypeStruct((N_TOK, D), BF16),
        grid=(N_TOK // TM,),
        in_specs=[pl.BlockSpec((TM, D), lambda i: (i, 0)),
                  pl.BlockSpec((1, 1, 3 * D), lambda i: (layer * 3 + _seg_of_tile(i, TM), 0, 1)),
                  pl.BlockSpec((1, 1, D), lambda i: (layer * 6 + 2, 0, 0))],
        out_specs=pl.BlockSpec((TM, D), lambda i: (i, 0)),
        compiler_params=_cparams(("parallel",)),
        name="hmod",
    )(x, mods, ng)


def _s5_scan(bur, bui, ar, ai, x0, n_seq, steps, d, store):
    def body(k, carry):
        xr, xi = carry
        t = k if d == 0 else steps - 1 - k
        r0 = pl.multiple_of(t * n_seq, n_seq)
        nr = ar * xr - ai * xi + bur[pl.ds(r0, n_seq), :]
        ni = ar * xi + ai * xr + bui[pl.ds(r0, n_seq), :]
        if store:
            bur[pl.ds(r0, n_seq), :] = nr
            bui[pl.ds(r0, n_seq), :] = ni
        return nr, ni

    return lax.fori_loop(0, steps, body, x0, unroll=8)


def _s5_seg_carries(er, ei, h0r, h0i, ar, ai, steps, d):
    pr, pi = ar, ai
    for _ in range(int(math.log2(steps))):
        pr, pi = pr * pr - pi * pi, 2.0 * pr * pi
    row = lax.broadcasted_iota(jnp.int32, er.shape, 0)
    n = S5_LAT_SEGS
    if d == 0:
        cr = jnp.where(row == 0, h0r, pltpu.roll(er, 1, 0))
        ci = jnp.where(row == 0, h0i, pltpu.roll(ei, 1, 0))
    else:
        cr = jnp.where(row == n - 1, h0r, pltpu.roll(er, n - 1, 0))
        ci = jnp.where(row == n - 1, h0i, pltpu.roll(ei, n - 1, 0))
    s = 1
    while s < n:
        keep = (row >= s) if d == 0 else (row + s <= n - 1)
        sh = s if d == 0 else n - s
        sr = jnp.where(keep, pltpu.roll(cr, sh, 0), 0.0)
        si = jnp.where(keep, pltpu.roll(ci, sh, 0), 0.0)
        cr, ci = cr + pr * sr - pi * si, ci + pr * si + pi * sr
        pr, pi = pr * pr - pi * pi, 2.0 * pr * pi
        s *= 2
    return cr, ci


def _s5_kernel(*refs, n_seq):
    if n_seq == S5_LAT_SEGS:
        u_ref, wb_ref, wc_ref, a_ref, h0_ref, y_ref, bur, bui = refs
    else:
        u_ref, wb_ref, wc_ref, a_ref, y_ref, fin_ref, bur, bui = refs
    steps = S5_ROWS // n_seq
    u = u_ref[0]
    y = None
    for d in range(2):
        bur[...] = _dot(u, wb_ref[d, 0, 0])
        bui[...] = _dot(u, wb_ref[d, 1, 0])
        ar = jnp.broadcast_to(a_ref[2 * d:2 * d + 1, :], (n_seq, S5_GW))
        ai = jnp.broadcast_to(a_ref[2 * d + 1:2 * d + 2, :], (n_seq, S5_GW))
        zero = jnp.zeros((n_seq, S5_GW), F32)
        if n_seq == S5_LAT_SEGS:
            er, ei = _s5_scan(bur, bui, ar, ai, (zero, zero), n_seq, steps, d, store=False)
            x0 = _s5_seg_carries(er, ei, h0_ref[0, 2 * d:2 * d + 1, :], h0_ref[0, 2 * d + 1:2 * d + 2, :],
                                 ar, ai, steps, d)
        else:
            x0 = (zero, zero)
        xr, xi = _s5_scan(bur, bui, ar, ai, x0, n_seq, steps, d, store=True)
        if n_seq != S5_LAT_SEGS:
            fin_ref[2 * d] = xr
            fin_ref[2 * d + 1] = xi
        yd = _dot(bur[...].astype(BF16), wc_ref[d, 0, 0]) - _dot(bui[...].astype(BF16), wc_ref[d, 1, 0])
        y = yd if y is None else y + yd
    y_ref[0] = y


def _s5(u_tm, wb, wc, abar, h0=None):
    n_sets = u_tm.shape[0]
    lat = h0 is not None
    n_seq = S5_LAT_SEGS if lat else CTX_B
    in_specs = [pl.BlockSpec((1, S5_ROWS, 128), lambda s, g: (s, 0, g)),
                pl.BlockSpec((2, 2, 1, 128, S5_GW), lambda s, g: (0, 0, g, 0, 0)),
                pl.BlockSpec((2, 2, 1, S5_GW, 128), lambda s, g: (0, 0, g, 0, 0)),
                pl.BlockSpec((4, S5_GW), lambda s, g: (0, g))]
    args = [u_tm, wb, wc, abar]
    out_shape = [jax.ShapeDtypeStruct((n_sets, S5_ROWS, D), F32)]
    out_specs = [pl.BlockSpec((1, S5_ROWS, 128), lambda s, g: (s, 0, g))]
    if lat:
        in_specs.append(pl.BlockSpec((1, 4, S5_GW), lambda s, g: (s, 0, g)))
        args.append(h0)
    else:
        out_shape.append(jax.ShapeDtypeStruct((4, CTX_B, S5_N), F32))
        out_specs.append(pl.BlockSpec((4, CTX_B, S5_GW), lambda s, g: (0, 0, g)))
    return pl.pallas_call(
        functools.partial(_s5_kernel, n_seq=n_seq),
        out_shape=out_shape,
        grid=(n_sets, S5_GROUPS // S5_GB),
        in_specs=in_specs,
        out_specs=out_specs,
        scratch_shapes=[pltpu.VMEM((S5_ROWS, S5_GW), F32), pltpu.VMEM((S5_ROWS, S5_GW), F32)],
        compiler_params=_cparams(("parallel", "parallel")),
        name="s5_lat" if lat else "s5_ctx",
    )(*args)


def _dn_prep_kernel(blk_ref, first_ref, last_ref, x_ref, prev_ref, next_ref, ba_ref, cw_ref, alog_ref, dtb_ref,
                    q_ref, k_ref, v_ref, g_ref):
    i = pl.program_id(0)
    nqkv = 2 * DN_NQK + DN_NV
    x = x_ref[...]
    pv = jnp.where(first_ref[i] == 1, 0.0, prev_ref[...])
    nx = jnp.where(last_ref[i] == 1, 0.0, next_ref[...])
    ext = jnp.concatenate([pv, x, nx], axis=0)
    n_ext = SEQ_BLK + 16
    acc = None
    for j in range(DN_CONV):
        off = DN_CONV // 2 - j
        e = ext if off == 0 else pltpu.roll(ext, off % n_ext, 0)
        term = cw_ref[j:j + 1, :] * e[8:8 + SEQ_BLK]
        acc = term if acc is None else acc + term
    a = _silu(acc)
    for h in range(DN_QK):
        cs = slice(h * DN_HD, (h + 1) * DN_HD)
        qh = a[:, cs]
        q_ref[:, cs] = qh * lax.rsqrt(jnp.sum(qh * qh, axis=-1, keepdims=True) + EPS) * (DN_HD ** -0.5)
        kh = a[:, DN_NQK + h * DN_HD:DN_NQK + (h + 1) * DN_HD]
        k_ref[:, cs] = kh * lax.rsqrt(jnp.sum(kh * kh, axis=-1, keepdims=True) + EPS)
    v_ref[...] = a[:, 2 * DN_NQK:nqkv]

    ba = ba_ref[...]
    lane = lax.broadcasted_iota(jnp.int32, ba.shape, 1)
    row = lax.broadcasted_iota(jnp.int32, ba.shape, 0) % DN_CHUNK
    is_g = ((lane % 16) >= 8) & (lane < 32)
    z = ba + dtb_ref[...]
    softplus = jnp.maximum(z, 0.0) + jnp.log1p(jnp.exp(-jnp.abs(z)))
    val = jnp.where(is_g, -jnp.exp(alog_ref[...]) * softplus, jax.nn.sigmoid(ba))
    cf = val
    cr = val
    s = 1
    while s < DN_CHUNK:
        cf = cf + jnp.where(row >= s, pltpu.roll(cf, s, 0), 0.0)
        cr = cr + jnp.where(row < DN_CHUNK - s, pltpu.roll(cr, SEQ_BLK - s, 0), 0.0)
        s *= 2
    g_ref[...] = jnp.where(lane < 32, jnp.where(is_g, jnp.where(lane < 16, cf, cr), val), 0.0)


def _dn_prep(proj, conv_w, alog_row, dtb_row, tables):
    fb, _, first, last, _ = tables
    nqkv = 2 * DN_NQK + DN_NV
    per = SEQ_BLK // 8
    n8 = N_TOK // 8
    grid_spec = pltpu.PrefetchScalarGridSpec(
        num_scalar_prefetch=3,
        grid=(fb.shape[0],),
        in_specs=[pl.BlockSpec((SEQ_BLK, nqkv), lambda i, blk, *_: (blk[i], 0)),
                  pl.BlockSpec((8, nqkv), lambda i, blk, *_: (jnp.maximum(blk[i] * per - 1, 0), 0)),
                  pl.BlockSpec((8, nqkv), lambda i, blk, *_: (jnp.minimum(blk[i] * per + per, n8 - 1), 0)),
                  pl.BlockSpec((SEQ_BLK, 128), lambda i, blk, *_: (blk[i], (nqkv + DN_NV) // 128)),
                  pl.BlockSpec((DN_CONV, nqkv), lambda i, *_: (0, 0)),
                  pl.BlockSpec((1, 128), lambda i, *_: (0, 0)),
                  pl.BlockSpec((1, 128), lambda i, *_: (0, 0))],
        out_specs=[pl.BlockSpec((SEQ_BLK, DN_NQK), lambda i, blk, *_: (blk[i], 0)),
                   pl.BlockSpec((SEQ_BLK, DN_NQK), lambda i, blk, *_: (blk[i], 0)),
                   pl.BlockSpec((SEQ_BLK, DN_NV), lambda i, blk, *_: (blk[i], 0)),
                   pl.BlockSpec((SEQ_BLK, 128), lambda i, blk, *_: (blk[i], 0))])
    return pl.pallas_call(
        _dn_prep_kernel,
        out_shape=[jax.ShapeDtypeStruct((N_TOK, DN_NQK), F32), jax.ShapeDtypeStruct((N_TOK, DN_NQK), F32),
                   jax.ShapeDtypeStruct((N_TOK, DN_NV), F32), jax.ShapeDtypeStruct((N_TOK, 128), F32)],
        grid_spec=grid_spec,
        compiler_params=_cparams(("arbitrary",)),
        name="dn_prep",
    )(fb, first, last, proj, proj, proj, proj, conv_w, alog_row, dtb_row)


DN_NCH = SEQ_BLK // DN_CHUNK


DN_LEVELS = int(math.log2(DN_CHUNK))


def _dn_setup(d, lane_beta, q, k, kk, qk, v, gates):
    n = SEQ_BLK
    lane = lax.broadcasted_iota(jnp.int32, gates.shape, 1)
    beta = jnp.sum(jnp.where(lane == lane_beta, gates, 0.0), axis=1, keepdims=True)
    gcol = jnp.sum(jnp.where(lane == lane_beta + 8, gates, 0.0), axis=1, keepdims=True)
    ri = lax.broadcasted_iota(jnp.int32, (n, n), 0)
    ci = lax.broadcasted_iota(jnp.int32, (n, n), 1)
    eye = ri == ci
    same = (ri // DN_CHUNK) == (ci // DN_CHUNK)
    grow = jnp.sum(jnp.where(eye, gcol, 0.0), axis=0, keepdims=True)
    if d == 0:
        lower, strict = same & (ri >= ci), same & (ri > ci)
        last_of = (ri // DN_CHUNK) * DN_CHUNK + DN_CHUNK - 1
    else:
        lower, strict = same & (ri <= ci), same & (ri < ci)
        last_of = (ri // DN_CHUNK) * DN_CHUNK
    decay = jnp.exp(jnp.where(lower, gcol - grow, NEG))
    g_last = jnp.sum(jnp.where(ci == last_of, grow, 0.0), axis=1, keepdims=True)
    return dict(
        d=d,
        lmat=jnp.where(strict, beta * kk * decay, 0.0),
        rhs16=jnp.concatenate([v * beta, k * (beta * jnp.exp(gcol))], axis=1).astype(BF16),
        aqk16=jnp.where(lower, qk * decay, 0.0).astype(BF16),
        qe16=(q * jnp.exp(gcol)).astype(BF16),
        kd16=(k * jnp.exp(g_last - gcol)).astype(BF16),
        eg=jnp.exp(g_last))


def _dn_solve_all(chains):
    n = SEQ_BLK
    ri = lax.broadcasted_iota(jnp.int32, (n, n), 0)
    ci = lax.broadcasted_iota(jnp.int32, (n, n), 1)
    diff = ri ^ ci
    level = sum((diff >= (1 << b)).astype(jnp.int32) for b in range(DN_LEVELS + 1))
    ts = [jnp.where(ri == ci, 1.0, 0.0) - jnp.where(level == 1, c["lmat"], 0.0) for c in chains]
    for lv in range(2, DN_LEVELS + 1):
        t16 = [t.astype(BF16) for t in ts]
        xs = [_dot(jnp.where(level == lv, c["lmat"], 0.0).astype(BF16), t) for c, t in zip(chains, t16)]
        ts = [t - _dot(th, x.astype(BF16)) for t, th, x in zip(ts, t16, xs)]
    return [_dot(t.astype(BF16), c["rhs16"]) for c, t in zip(chains, ts)]


def _dn_scan_all(chains, sols, states):
    us = [s[:, :DN_HD] for s in sols]
    w16 = [s[:, DN_HD:].astype(BF16) for s in sols]
    outs = [[None] * DN_NCH for _ in chains]
    for step in range(DN_NCH):
        idx = [step if c["d"] == 0 else DN_NCH - 1 - step for c in chains]
        rows = [slice(i * DN_CHUNK, (i + 1) * DN_CHUNK) for i in idx]
        s16 = [s.astype(BF16) for s in states]
        vn16 = [(u[r] - _dot(w[r], s)).astype(BF16) for u, w, r, s in zip(us, w16, rows, s16)]
        for n_, (c, r, s, vn) in enumerate(zip(chains, rows, s16, vn16)):
            outs[n_][idx[n_]] = _dot(c["qe16"][r], s) + _dot(c["aqk16"][r, r], vn)
        states = [s * c["eg"][r.start:r.start + 1] + _dot_tn(c["kd16"][r], vn)
                  for s, c, r, vn in zip(states, chains, rows, vn16)]
    return [jnp.concatenate(o, axis=0) for o in outs], states


def _dn_kernel(fb_ref, bb_ref, first_ref, last_ref, sid_ref,
               qf_ref, kf_ref, vf_ref, gf_ref, qb_ref, kb_ref, vb_ref, gb_ref, s0_ref,
               of_ref, ob_ref, so_ref, s_ref):
    j = pl.program_id(0)
    i = pl.program_id(1)
    is_first = first_ref[i] == 1
    is_lat = sid_ref[i] >= CTX_B

    @pl.when(is_first & is_lat)
    def _():
        s_ref[...] = s0_ref[0]

    @pl.when(is_first & jnp.logical_not(is_lat))
    def _():
        s_ref[...] = jnp.zeros_like(s_ref)

    chains = []
    for d in range(2):
        q_ref, k_ref, v_ref, g_ref = (qf_ref, kf_ref, vf_ref, gf_ref) if d == 0 else (qb_ref, kb_ref, vb_ref, gb_ref)
        q, k, gates = q_ref[...], k_ref[...], g_ref[...]
        k16 = k.astype(BF16)
        kk = _dot_nt(k16, k16)
        qk = _dot_nt(q.astype(BF16), k16)
        for e in range(DN_REP):
            chains.append(_dn_setup(d, 16 * d + DN_REP * j + e, q, k, kk, qk,
                                    v_ref[:, e * DN_HD:(e + 1) * DN_HD], gates))
    sols = _dn_solve_all(chains)
    outs, states = _dn_scan_all(chains, sols, [s_ref[d, e] for d in range(2) for e in range(DN_REP)])
    for n_, (o, s_new) in enumerate(zip(outs, states)):
        d, e = divmod(n_, DN_REP)
        (of_ref if d == 0 else ob_ref)[:, e * DN_HD:(e + 1) * DN_HD] = o
        s_ref[d, e] = s_new
    so_ref[0] = s_ref[...]


def _dn(qn, kn, vv, gates, state, tables):
    n_steps = tables[0].shape[0]
    n_seq = CTX_B + LAT_B

    def fwd(col):
        return lambda j, i, fb, *_: (fb[i], col(j))

    def bwd(col):
        return lambda j, i, fb, bb, *_: (bb[i], col(j))

    head = lambda j: j
    gate = lambda j: 0

    def st_in(j, i, fb, bb, first, last, sid):
        return (jnp.maximum(sid[i] - CTX_B, 0), 0, j, 0, 0)

    def st_out(j, i, fb, bb, first, last, sid):
        return (sid[i], 0, j, 0, 0)

    blk = lambda w: (SEQ_BLK, w)
    vw = DN_REP * DN_HD
    grid_spec = pltpu.PrefetchScalarGridSpec(
        num_scalar_prefetch=5,
        grid=(DN_QK, n_steps),
        in_specs=[pl.BlockSpec(blk(DN_HD), fwd(head)), pl.BlockSpec(blk(DN_HD), fwd(head)),
                  pl.BlockSpec(blk(vw), fwd(head)), pl.BlockSpec(blk(128), fwd(gate)),
                  pl.BlockSpec(blk(DN_HD), bwd(head)), pl.BlockSpec(blk(DN_HD), bwd(head)),
                  pl.BlockSpec(blk(vw), bwd(head)), pl.BlockSpec(blk(128), bwd(gate)),
                  pl.BlockSpec((1, 2, DN_REP, DN_HD, DN_HD), st_in)],
        out_specs=[pl.BlockSpec(blk(vw), fwd(head)), pl.BlockSpec(blk(vw), bwd(head)),
                   pl.BlockSpec((1, 2, DN_REP, DN_HD, DN_HD), st_out)],
        scratch_shapes=[pltpu.VMEM((2, DN_REP, DN_HD, DN_HD), F32)])
    return pl.pallas_call(
        _dn_kernel,
        out_shape=[jax.ShapeDtypeStruct((N_TOK, DN_NV), F32), jax.ShapeDtypeStruct((N_TOK, DN_NV), F32),
                   jax.ShapeDtypeStruct((n_seq, 2, DN_V, DN_HD, DN_HD), F32)],
        grid_spec=grid_spec,
        compiler_params=_cparams(("arbitrary", "arbitrary")),
        name="dn",
    )(*tables, qn, kn, vv, gates, qn, kn, vv, gates, state)


def kernel(x_prompt, x_sample, cache_attn_k, cache_attn_v, state_s5_re, state_s5_im, cache_na_k, cache_na_v,
           state_dn, c, c_ctx, norm_g, w_ada, b_ada, ffn_w_gu, ffn_w_d, a_w_qkv, a_w_o, a_sink,
           s5_lam_re, s5_lam_im, s5_log_dt, s5_b_re, s5_b_im, s5_c_re, s5_c_im, s5_d, s5_w_glu,
           na_w_qkv, na_w_o, na_rpb, dn_w_in, dn_conv_w, dn_w_ba, dn_a_log, dn_dt_bias, dn_out_g, dn_w_o):
    depth = w_ada.shape[0]
    cond8 = jnp.concatenate([c_ctx[None, :], c, jnp.zeros((8 - 1 - LAT_B, D), F32)], axis=0)
    mods = _ada(cond8, w_ada, b_ada)[:, :1 + LAT_B].reshape(depth * (1 + LAT_B), 1, 9 * D)
    ng = norm_g.reshape(depth * 6, 1, D)
    wgu = ffn_w_gu.astype(BF16)
    wd = ffn_w_d.astype(BF16)
    tables = _seq_tables()

    x = _ffn([x_prompt.reshape(N_CTX, D), x_sample.reshape(N_LAT, D)], mods, ng, wgu, wd, 0, 0)
    qkv = _proj(x, mods, ng, a_w_qkv[0].astype(BF16), 0)
    qw, kw = A_HEADS * HEAD, A_KV * HEAD
    cos, sin = _rope_tables()
    o_ctx = _attn_ctx(qkv, A_HEADS, A_KV, a_sink[0])
    o_lat = _attn_a_lat(qkv, a_sink[0], cache_attn_k[:, 0].reshape(LAT_B, PAST, kw),
                        cache_attn_v[:, 0].reshape(LAT_B, PAST, kw), cos, sin)
    new_attn_k = qkv[:N_CTX, qw:qw + kw].reshape(CTX_B, 1, CTX_L, A_KV, HEAD)
    new_attn_v = qkv[:N_CTX, qw + kw:].reshape(CTX_B, 1, CTX_L, A_KV, HEAD)
    x = _out_pair(o_ctx, o_lat, a_w_o[0].astype(BF16), x, mods, ng, 0)
    x = _ffn([x], mods, ng, wgu, wd, 0, 1)

    x = _ffn([x], mods, ng, wgu, wd, 1, 0)
    wb, wc, abar = _s5_tables(s5_lam_re[0], s5_lam_im[0], s5_log_dt[0], s5_b_re[0], s5_b_im[0],
                              s5_c_re[0], s5_c_im[0])
    hb = _hmod(x, mods, ng, 1)
    seg = LAT_L // S5_LAT_SEGS
    u_ctx = hb[:N_CTX].reshape(CTX_B, CTX_L, D).transpose(1, 0, 2).reshape(1, N_CTX, D)
    u_lat = hb[N_CTX:].reshape(LAT_B, S5_LAT_SEGS, seg, D).transpose(0, 2, 1, 3).reshape(LAT_B, LAT_L, D)
    h0 = jnp.stack([state_s5_re[:, 0], state_s5_im[:, 0]], axis=2).reshape(LAT_B, 4, S5_N)
    y_ctx, fin = _s5(u_ctx, wb, wc, abar)
    y_lat, = _s5(u_lat, wb, wc, abar, h0)
    y_ctx = y_ctx.reshape(CTX_L, CTX_B, D).transpose(1, 0, 2).reshape(N_CTX, D)
    y_lat = y_lat.reshape(LAT_B, seg, S5_LAT_SEGS, D).transpose(0, 2, 1, 3).reshape(N_LAT, D)
    fin = fin.reshape(2, 2, CTX_B, S5_GROUPS, S5_STATE)
    new_s5_re = jnp.transpose(fin[:, 0], (1, 0, 2, 3))[:, None]
    new_s5_im = jnp.transpose(fin[:, 1], (1, 0, 2, 3))[:, None]
    x = _out_s5(y_ctx, y_lat, s5_d, s5_w_glu[0].astype(BF16), x, mods, ng, 1)
    x = _ffn([x], mods, ng, wgu, wd, 1, 1)

    x = _ffn([x], mods, ng, wgu, wd, 2, 0)
    qkv = _proj(x, mods, ng, na_w_qkv[0].astype(BF16), 2)
    cw = C_HEADS * HEAD
    o_ctx = _attn_ctx(qkv, C_HEADS, C_HEADS)
    o_lat = _attn_c_lat(qkv, qkv[N_CTX:, cw:].astype(BF16), cache_na_k[:, 0].reshape(LAT_B, PAST, cw),
                        cache_na_v[:, 0].reshape(LAT_B, PAST, cw), *_na_tables(na_rpb[0]))
    new_na_k = qkv[:N_CTX, cw:2 * cw].reshape(CTX_B, 1, CTX_L, C_HEADS, HEAD)
    new_na_v = qkv[:N_CTX, 2 * cw:].reshape(CTX_B, 1, CTX_L, C_HEADS, HEAD)
    x = _out_pair(o_ctx, o_lat, na_w_o[0].astype(BF16), x, mods, ng, 2)
    x = _ffn([x], mods, ng, wgu, wd, 2, 1)

    x = _ffn([x], mods, ng, wgu, wd, 3, 0)
    w_all = jnp.concatenate([dn_w_in[0], dn_w_ba[0, 0], dn_w_ba[0, 1],
                             jnp.zeros((D, 128 - 4 * DN_V), F32)], axis=1).astype(BF16)
    proj = _proj(x, mods, ng, w_all, 3)
    pad8 = jnp.zeros((DN_V,), F32)
    gate_row = lambda t: jnp.concatenate([pad8, t[0], pad8, t[1], jnp.zeros((128 - 4 * DN_V,), F32)])[None, :]
    qn, kn, vv, gates = _dn_prep(proj, dn_conv_w[0], gate_row(dn_a_log[0]), gate_row(dn_dt_bias[0]), tables)
    of, ob, fin_dn = _dn(qn, kn, vv, gates, state_dn[:, 0], tables)
    new_dn = fin_dn[:CTX_B][:, None]
    x = _out_dn(of, ob, proj, dn_out_g, dn_w_o[0].astype(BF16), x, mods, ng, 3)
    y, z = _ffn([x], mods, ng, wgu, wd, 3, 1, split_out=True)
    y = y.reshape(CTX_B, CTX_L, D)
    z = z.reshape(LAT_B, LAT_L, D)
    return (y, z, new_attn_k, new_attn_v, new_s5_re, new_s5_im, new_na_k, new_na_v, new_dn)
```

```python
import functools
import math

import numpy as np
import jax
import jax.numpy as jnp
from jax import lax
from jax.experimental import pallas as pl
from jax.experimental.pallas import tpu as pltpu

F32 = jnp.float32
BF16 = jnp.bfloat16

D = 1024
CTX_B, CTX_L = 16, 256
LAT_B, LAT_L = 2, 4096
N_CTX = CTX_B * CTX_L
N_LAT = LAT_B * LAT_L
N_TOK = N_CTX + N_LAT
PAST = 512
D_FF = 2816
EPS = 1e-6
NEG = -1e30
GRID_W = 64
HEAD = 64
A_HEADS, A_KV = 16, 4
A_WIN = 128
C_HEADS = 16
NA_ROWS, NA_COLS = 8, 16
ROPE_BASE = 10000.0
S5_GROUPS, S5_GROUP, S5_STATE = 64, 16, 64
S5_N = S5_GROUPS * S5_STATE
DN_QK, DN_V, DN_HD, DN_CONV, DN_CHUNK = 4, 8, 128, 5, 64
DN_REP = DN_V // DN_QK
DN_NQK = DN_QK * DN_HD
DN_NV = DN_V * DN_HD
DN_PROJ = 2 * DN_NQK + 2 * DN_NV + 128

TM = 512
SEQ_BLK = 256
VMEM_MB = 56
ADA_TN = 1152


def _cparams(sem, mb=VMEM_MB):
    return pltpu.CompilerParams(dimension_semantics=sem, vmem_limit_bytes=mb * 1024 * 1024)


def _resident(block, index_map):
    return pl.BlockSpec(block, index_map, pipeline_mode=pl.Buffered(1))


def _seg_of_tile(i, tm):
    nct = N_CTX // tm
    return jnp.where(i < nct, 0, 1 + (i - nct) // (LAT_L // tm))


def _dot(a, b):
    return jnp.dot(a, b, preferred_element_type=F32)


def _dot_nt(a, b):
    return lax.dot_general(a, b, (((1,), (1,)), ((), ())), preferred_element_type=F32)


def _dot_tn(a, b):
    return lax.dot_general(a, b, (((0,), (0,)), ((), ())), preferred_element_type=F32)


def _silu(x):
    return x * jax.nn.sigmoid(x)


def _modulate(x, g, m):
    xn = x * lax.rsqrt(jnp.mean(x * x, axis=-1, keepdims=True) + EPS) * g
    return xn * (1.0 + m[:, D:2 * D]) + m[:, :D]


def _residual(x, y, g, m, weight):
    yn = y * lax.rsqrt(jnp.mean(y * y, axis=-1, keepdims=True) + EPS) * g
    return x + weight * m[:, 2 * D:] * yn


def _ada_kernel(cond_ref, w_ref, b_ref, o_ref):
    s = _silu(cond_ref[...]).astype(BF16)
    o_ref[0] = _dot(s, w_ref[0].astype(BF16)) + b_ref[0]


def _ada(cond8, w_ada, b_ada):
    depth, _, n9 = w_ada.shape
    return pl.pallas_call(
        _ada_kernel,
        out_shape=jax.ShapeDtypeStruct((depth, 8, n9), F32),
        grid=(depth, n9 // ADA_TN),
        in_specs=[pl.BlockSpec((8, D), lambda l, j: (0, 0)),
                  pl.BlockSpec((1, D, ADA_TN), lambda l, j: (l, 0, j)),
                  pl.BlockSpec((1, 1, ADA_TN), lambda l, j: (l, 0, j))],
        out_specs=pl.BlockSpec((1, 8, ADA_TN), lambda l, j: (l, 0, j)),
        compiler_params=_cparams(("parallel", "parallel")),
        name="ada",
    )(cond8, w_ada, b_ada.reshape(depth, 1, n9))


FF_CW = 256


def _ffn_input(head, refs, is_ctx):
    if head == "plain":
        return refs[0][...]
    if head == "split":
        return jnp.where(is_ctx, refs[0][...], refs[1][...])
    x_ref, mod_ref, gpost_ref = refs[-3:]
    x, m = x_ref[...], mod_ref[0]
    if head == "pair":
        actx_ref, alat_ref, w_ref = refs[:3]
        a = jnp.where(is_ctx, actx_ref[...], alat_ref[...]).astype(BF16)
        y = _dot(a, w_ref[...])
    elif head == "s5":
        yc_ref, yl_ref, dsk_ref, gpre_ref, w_ref = refs[:5]
        y = jnp.where(is_ctx, yc_ref[...], yl_ref[...]) + dsk_ref[...] * _modulate(x, gpre_ref[0], m)
        t = _dot(jax.nn.gelu(y).astype(BF16), w_ref[...])
        y = t[:, :D] * jax.nn.sigmoid(t[:, D:])
    else:
        of_ref, ob_ref, z_ref, og_ref, w_ref = refs[:5]
        o = of_ref[...] + ob_ref[...]
        og = og_ref[...]
        heads = []
        for h in range(DN_V):
            cs = slice(h * DN_HD, (h + 1) * DN_HD)
            oh = o[:, cs]
            on = oh * lax.rsqrt(jnp.mean(oh * oh, axis=-1, keepdims=True) + EPS) * og
            heads.append((on * _silu(z_ref[:, cs])).astype(BF16))
        y = _dot(jnp.concatenate(heads, axis=1), w_ref[...])
    return _residual(x, y, gpost_ref[0], m, 1.0)


def _ffn_kernel(*refs, head, n_head, n_out):
    mod_ref, gpre_ref, gpost_ref, wgu_ref, wd_ref = refs[n_head:n_head + 5]
    o_refs, act_ref = refs[n_head + 5:n_head + 5 + n_out], refs[-1]
    is_ctx = pl.program_id(0) < N_CTX // TM
    x = _ffn_input(head, refs[:n_head], is_ctx)
    m = mod_ref[0]
    h = _modulate(x, gpre_ref[0], m).astype(BF16)
    for c in range(D_FF // FF_CW):
        g = _dot(h, wgu_ref[:, c * FF_CW:(c + 1) * FF_CW])
        u = _dot(h, wgu_ref[:, D_FF + c * FF_CW:D_FF + (c + 1) * FF_CW])
        act_ref[:, c * FF_CW:(c + 1) * FF_CW] = (_silu(g) * u).astype(BF16)
    y = _dot(act_ref[...], wd_ref[...])
    out = _residual(x, y, gpost_ref[0], m, 0.5)
    if n_out == 1:
        o_refs[0][...] = out
    else:
        @pl.when(is_ctx)
        def _():
            o_refs[0][...] = out

        @pl.when(jnp.logical_not(is_ctx))
        def _():
            o_refs[1][...] = out


def _ffn(xs, mods, ng, wgu, wd, layer, j, head="plain", split_out=False):
    s = 2 * j
    nct = N_CTX // TM
    ctx_spec = pl.BlockSpec((TM, D), lambda i: (jnp.minimum(i, nct - 1), 0))
    lat_spec = pl.BlockSpec((TM, D), lambda i: (jnp.maximum(i - nct, 0), 0))
    all_spec = pl.BlockSpec((TM, D), lambda i: (i, 0))
    row_spec = lambda k: pl.BlockSpec((1, 1, D), lambda i: (layer * 6 + k, 0, 0))
    mod_spec = lambda sub: pl.BlockSpec((1, 1, 3 * D), lambda i: (layer * 3 + _seg_of_tile(i, TM), 0, sub))
    whole = lambda a: _resident(a.shape, lambda i: (0,) * a.ndim)
    if head == "plain":
        head_specs = [all_spec]
    elif head == "split":
        head_specs = [ctx_spec, lat_spec]
    else:
        if head == "pair":
            head_specs = [ctx_spec, lat_spec, whole(xs[2])]
        elif head == "s5":
            head_specs = [ctx_spec, lat_spec, pl.BlockSpec((1, D), lambda i: (0, 0)), row_spec(2), whole(xs[3])]
            xs = xs[:3] + [ng] + xs[3:]
        else:
            zblk = (2 * DN_NQK + DN_NV) // DN_NV
            head_specs = [all_spec, all_spec, pl.BlockSpec((TM, DN_NV), lambda i: (i, zblk)),
                          pl.BlockSpec((1, DN_HD), lambda i: (0, 0)), whole(xs[4])]
        head_specs += [all_spec, mod_spec(1), row_spec(3)]
        xs = xs + [mods, ng]
    if split_out:
        out_shape = [jax.ShapeDtypeStruct((N_CTX, D), F32), jax.ShapeDtypeStruct((N_LAT, D), F32)]
        out_specs = [ctx_spec, lat_spec]
    else:
        out_shape, out_specs = jax.ShapeDtypeStruct((N_TOK, D), F32), all_spec
    return pl.pallas_call(
        functools.partial(_ffn_kernel, head=head, n_head=len(xs), n_out=2 if split_out else 1),
        out_shape=out_shape,
        grid=(N_TOK // TM,),
        in_specs=head_specs
                 + [pl.BlockSpec((1, 1, 3 * D), lambda i: (layer * 3 + _seg_of_tile(i, TM), 0, s)),
                    pl.BlockSpec((1, 1, D), lambda i: (layer * 6 + 2 * s, 0, 0)),
                    pl.BlockSpec((1, 1, D), lambda i: (layer * 6 + 2 * s + 1, 0, 0)),
                    _resident((None, None, D, 2 * D_FF), lambda i: (layer, j, 0, 0)),
                    _resident((None, None, D_FF, D), lambda i: (layer, j, 0, 0))],
        out_specs=out_specs,
        scratch_shapes=[pltpu.VMEM((TM, D_FF), BF16)],
        compiler_params=_cparams(("arbitrary",)),
        name="ffn",
    )(*xs, mods, ng, ng, wgu, wd)


def _proj_kernel(x_ref, mod_ref, g_ref, w_ref, o_ref, *kv_refs, kv_cols):
    h = _modulate(x_ref[...], g_ref[0], mod_ref[0]).astype(BF16)
    p = _dot(h, w_ref[...])
    o_ref[...] = p
    if kv_refs:
        @pl.when(pl.program_id(0) < N_CTX // TM)
        def _():
            for ref, (c0, c1) in zip(kv_refs, kv_cols):
                ref[...] = p[:, c0:c1]


def _proj(x, mods, ng, w, layer, kv_cols=()):
    n = w.shape[1]
    nct = N_CTX // TM
    out_shape = [jax.ShapeDtypeStruct((N_TOK, n), F32)]
    out_specs = [pl.BlockSpec((TM, n), lambda i: (i, 0))]
    for c0, c1 in kv_cols:
        out_shape.append(jax.ShapeDtypeStruct((N_CTX, c1 - c0), F32))
        out_specs.append(pl.BlockSpec((TM, c1 - c0), lambda i: (jnp.minimum(i, nct - 1), 0)))
    return pl.pallas_call(
        functools.partial(_proj_kernel, kv_cols=tuple(kv_cols)),
        out_shape=out_shape,
        grid=(N_TOK // TM,),
        in_specs=[pl.BlockSpec((TM, D), lambda i: (i, 0)),
                  pl.BlockSpec((1, 1, 3 * D), lambda i: (layer * 3 + _seg_of_tile(i, TM), 0, 1)),
                  pl.BlockSpec((1, 1, D), lambda i: (layer * 6 + 2, 0, 0)),
                  _resident((D, n), lambda i: (0, 0))],
        out_specs=out_specs,
        compiler_params=_cparams(("arbitrary",)),
        name="proj",
    )(x, mods, ng, w)


def _softmax_pv(parts, sink):
    m = functools.reduce(jnp.maximum, [jnp.max(s, axis=-1, keepdims=True) for s, _ in parts])
    if sink is not None:
        m = jnp.maximum(m, sink)
    l = None
    o = None
    for s, v in parts:
        p = jnp.exp(s - m)
        ls = jnp.sum(p, axis=-1, keepdims=True)
        os_ = _dot(p.astype(BF16), v)
        l = ls if l is None else l + ls
        o = os_ if o is None else o + os_
    if sink is not None:
        l = l + jnp.exp(sink - m)
    return o / l


def _attn_ctx_kernel(*refs, n_heads, group, has_sink):
    if has_sink:
        sink_ref, q_ref, k_ref, v_ref, o_ref = refs
    else:
        q_ref, k_ref, v_ref, o_ref = refs
    scale = HEAD ** -0.5
    for h in range(n_heads):
        kv = h // group
        q = (q_ref[:, h * HEAD:(h + 1) * HEAD] * scale).astype(BF16)
        k = k_ref[:, kv * HEAD:(kv + 1) * HEAD].astype(BF16)
        v = v_ref[:, kv * HEAD:(kv + 1) * HEAD].astype(BF16)
        s = _dot_nt(q, k)
        o = _softmax_pv([(s, v)], sink_ref[h] if has_sink else None)
        o_ref[:, h * HEAD:(h + 1) * HEAD] = o


def _attn_ctx(qkv, n_heads, n_kv, sink=None):
    qw, kw = n_heads * HEAD, n_kv * HEAD
    kern = functools.partial(_attn_ctx_kernel, n_heads=n_heads, group=n_heads // n_kv,
                             has_sink=sink is not None)
    specs = [pl.BlockSpec((CTX_L, qw), lambda b: (b, 0)),
             pl.BlockSpec((CTX_L, kw), lambda b: (b, qw // kw)),
             pl.BlockSpec((CTX_L, kw), lambda b: (b, qw // kw + 1))]
    args = [qkv, qkv, qkv]
    if sink is not None:
        specs = [pl.BlockSpec(memory_space=pltpu.SMEM)] + specs
        args = [sink] + args
    return pl.pallas_call(
        kern,
        out_shape=jax.ShapeDtypeStruct((N_CTX, qw), F32),
        grid=(CTX_B,),
        in_specs=specs,
        out_specs=pl.BlockSpec((CTX_L, qw), lambda b: (b, 0)),
        compiler_params=_cparams(("parallel",)),
        name="attn_ctx",
    )(*args)


def _rope_tables():
    n = HEAD // 4
    inv = ROPE_BASE ** (-jnp.arange(n, dtype=F32) / n)
    t = jnp.arange(LAT_L)
    ang_r = (t // GRID_W).astype(F32)[:, None] * inv[None, :]
    ang_c = (t % GRID_W).astype(F32)[:, None] * inv[None, :]
    cos = jnp.concatenate([jnp.cos(ang_r), jnp.cos(ang_r), jnp.cos(ang_c), jnp.cos(ang_c)], axis=-1)
    sin = jnp.concatenate([-jnp.sin(ang_r), jnp.sin(ang_r), -jnp.sin(ang_c), jnp.sin(ang_c)], axis=-1)
    return jnp.tile(cos, (1, 2)), jnp.tile(sin, (1, 2))


def _rope(x, cos, sin):
    rows = x.shape[0]
    lane = lax.broadcasted_iota(jnp.int32, (rows, 128), 1)
    first = (lane % 32) < 16
    outs = []
    for cb in range(x.shape[1] // 128):
        xb = x[:, cb * 128:(cb + 1) * 128]
        partner = jnp.where(first, pltpu.roll(xb, 112, 1), pltpu.roll(xb, 16, 1))
        outs.append(xb * cos + partner * sin)
    return jnp.concatenate(outs, axis=1)


A_KWIN = 3 * A_WIN


def _attn_a_lat_kernel(sink_ref, q_ref, k_ref, v_ref, kc_ref, vc_ref, cos_ref, sin_ref, o_ref):
    n = pl.program_id(1)
    nb = LAT_L // A_WIN
    start = pl.multiple_of(jnp.clip(n - 1, 0, nb - 3) * A_WIN, A_WIN)
    q0 = pl.multiple_of(n * A_WIN, A_WIN)
    scale = HEAD ** -0.5
    q = _rope(q_ref[...], cos_ref[pl.ds(q0, A_WIN), :], sin_ref[pl.ds(q0, A_WIN), :]) * scale
    k = _rope(k_ref[pl.ds(start, A_KWIN), :], cos_ref[pl.ds(start, A_KWIN), :],
              sin_ref[pl.ds(start, A_KWIN), :]).astype(BF16)
    v = v_ref[pl.ds(start, A_KWIN), :].astype(BF16)
    kc = kc_ref[0].astype(BF16)
    vc = vc_ref[0].astype(BF16)
    qpos = q0 + lax.broadcasted_iota(jnp.int32, (A_WIN, A_KWIN), 0)
    kpos = start + lax.broadcasted_iota(jnp.int32, (A_WIN, A_KWIN), 1)
    ok = jnp.abs(kpos - qpos) <= A_WIN
    group = A_HEADS // A_KV
    for h in range(A_HEADS):
        kvs = slice((h // group) * HEAD, (h // group + 1) * HEAD)
        qh = q[:, h * HEAD:(h + 1) * HEAD].astype(BF16)
        s_loc = jnp.where(ok, _dot_nt(qh, k[:, kvs]), NEG)
        s_ctx = _dot_nt(qh, kc[:, kvs])
        o = _softmax_pv([(s_loc, v[:, kvs]), (s_ctx, vc[:, kvs])], sink_ref[h])
        o_ref[:, h * HEAD:(h + 1) * HEAD] = o


def _attn_a_lat(qkv, sink, kc, vc, cos, sin):
    qw, kw = A_HEADS * HEAD, A_KV * HEAD
    nb = LAT_L // A_WIN
    return pl.pallas_call(
        _attn_a_lat_kernel,
        out_shape=jax.ShapeDtypeStruct((N_LAT, qw), F32),
        grid=(LAT_B, nb),
        in_specs=[pl.BlockSpec(memory_space=pltpu.SMEM),
                  pl.BlockSpec((A_WIN, qw), lambda b, n: (N_CTX // A_WIN + b * nb + n, 0)),
                  pl.BlockSpec((LAT_L, kw), lambda b, n: (N_CTX // LAT_L + b, qw // kw)),
                  pl.BlockSpec((LAT_L, kw), lambda b, n: (N_CTX // LAT_L + b, qw // kw + 1)),
                  pl.BlockSpec((1, PAST, kw), lambda b, n: (b, 0, 0)),
                  pl.BlockSpec((1, PAST, kw), lambda b, n: (b, 0, 0)),
                  _resident((LAT_L, 128), lambda b, n: (0, 0)),
                  _resident((LAT_L, 128), lambda b, n: (0, 0))],
        out_specs=pl.BlockSpec((A_WIN, qw), lambda b, n: (b * nb + n, 0)),
        compiler_params=_cparams(("parallel", "parallel")),
        name="attn_a_lat",
    )(sink, qkv, qkv, qkv, kc, vc, cos, sin)


NA_QROWS = 8
NA_QTOK = NA_QROWS * GRID_W
NA_KROWS = NA_QROWS + NA_ROWS
NA_KTOK = NA_KROWS * GRID_W
NA_KBLK = (NA_ROWS // 2) * GRID_W
NA_NKB = NA_KTOK // NA_KBLK
NA_HPAIR = 2


def _na_tables(rpb):
    col = jnp.arange(GRID_W)
    dcol = jnp.clip(col[None, :] - col[:, None], 1 - NA_COLS, NA_COLS - 1) + NA_COLS - 1
    cs = jnp.clip(col - NA_COLS // 2, 0, GRID_W - NA_COLS)
    okc = (col[None, :] >= cs[:, None]) & (col[None, :] < cs[:, None] + NA_COLS)
    bias = jnp.where(okc[None, None], rpb[:, :, dcol], NEG)
    a = jnp.arange(NA_QROWS)[:, None]
    j = jnp.arange(NA_KROWS)[None, :]
    rows = LAT_L // GRID_W
    masks = []
    for blk in (0, 1, rows // NA_QROWS - 1):
        r = blk * NA_QROWS + a
        key_row = blk * NA_QROWS - NA_ROWS // 2 + j
        rs = jnp.clip(r - NA_ROWS // 2, 0, rows - NA_ROWS)
        valid = (key_row >= rs) & (key_row < rs + NA_ROWS)
        m = jnp.where(valid, 0.0, NEG).astype(F32)
        masks.append(jnp.broadcast_to(m[:, None, :, None], (NA_QROWS, GRID_W, NA_KROWS, GRID_W))
                     .reshape(NA_QTOK, NA_KTOK))
    return bias, jnp.stack(masks)


def _attn_c_lat_kernel(*refs):
    q_ref = refs[0]
    k_refs = refs[1:1 + NA_NKB]
    v_refs = refs[1 + NA_NKB:1 + 2 * NA_NKB]
    kc_ref, vc_ref, tcol_ref, wm_ref, o_ref, bias_ref = refs[1 + 2 * NA_NKB:]
    i = pl.program_id(2)
    nblk = LAT_L // NA_QTOK

    @pl.when((pl.program_id(1) == 0) & (i == 0))
    def _():
        neg = jnp.full((GRID_W, GRID_W), NEG, F32)
        for e in range(NA_HPAIR):
            for a in range(NA_QROWS):
                tiles = []
                for j in range(NA_KROWS):
                    d = j - NA_ROWS // 2 - a + NA_ROWS - 1
                    tiles.append(tcol_ref[e, d] if 0 <= d < 2 * NA_ROWS - 1 else neg)
                bias_ref[e, a * GRID_W:(a + 1) * GRID_W, :] = jnp.concatenate(tiles, axis=1)

    var = jnp.where(i == 0, 0, jnp.where(i == nblk - 1, 2, 1))
    scale = HEAD ** -0.5
    for e in range(NA_HPAIR):
        hs = slice(e * HEAD, (e + 1) * HEAD)
        qh = (q_ref[:, hs] * scale).astype(BF16)
        parts = []
        for m in range(NA_NKB):
            cs = slice(m * NA_KBLK, (m + 1) * NA_KBLK)
            s = _dot_nt(qh, k_refs[m][:, hs].astype(BF16)) + bias_ref[e, :, cs] + wm_ref[var, :, cs]
            parts.append((s, v_refs[m][:, hs].astype(BF16)))
        parts.append((_dot_nt(qh, kc_ref[0, :, hs].astype(BF16)), vc_ref[0, :, hs].astype(BF16)))
        o_ref[:, hs] = _softmax_pv(parts, None)


def _attn_c_lat(qkv, kc, vc, tcol, wmask):
    qw = C_HEADS * HEAD
    hw = NA_HPAIR * HEAD
    nblk = LAT_L // NA_QTOK
    nkb = LAT_L // NA_KBLK
    per = NA_QTOK // NA_KBLK

    def kv_spec(m, col0):
        def idx(hp, b, i):
            return (N_CTX // NA_KBLK + b * nkb + jnp.clip(i * per - 1 + m, 0, nkb - 1), col0 + hp)
        return pl.BlockSpec((NA_KBLK, hw), idx)

    return pl.pallas_call(
        _attn_c_lat_kernel,
        out_shape=jax.ShapeDtypeStruct((N_LAT, qw), F32),
        grid=(C_HEADS // NA_HPAIR, LAT_B, nblk),
        in_specs=[pl.BlockSpec((NA_QTOK, hw), lambda hp, b, i: (N_CTX // NA_QTOK + b * nblk + i, hp))]
                 + [kv_spec(m, qw // hw) for m in range(NA_NKB)]
                 + [kv_spec(m, 2 * qw // hw) for m in range(NA_NKB)]
                 + [pl.BlockSpec((1, PAST, hw), lambda hp, b, i: (b, 0, hp)),
                    pl.BlockSpec((1, PAST, hw), lambda hp, b, i: (b, 0, hp)),
                    pl.BlockSpec((NA_HPAIR, 2 * NA_ROWS - 1, GRID_W, GRID_W), lambda hp, b, i: (hp, 0, 0, 0)),
                    _resident((3, NA_QTOK, NA_KTOK), lambda hp, b, i: (0, 0, 0))],
        out_specs=pl.BlockSpec((NA_QTOK, hw), lambda hp, b, i: (b * nblk + i, hp)),
        scratch_shapes=[pltpu.VMEM((NA_HPAIR, NA_QTOK, NA_KTOK), F32)],
        compiler_params=_cparams(("arbitrary", "arbitrary", "arbitrary")),
        name="attn_c_lat",
    )(qkv, *([qkv] * (2 * NA_NKB)), kc, vc, tcol, wmask)


def _seq_tables():
    fb, bb, first, last, sid = [], [], [], [], []
    base = 0
    for s, length in enumerate([CTX_L] * CTX_B + [LAT_L] * LAT_B):
        n = length // SEQ_BLK
        for c in range(n):
            fb.append(base + c)
            bb.append(base + n - 1 - c)
            first.append(int(c == 0))
            last.append(int(c == n - 1))
            sid.append(s)
        base += n
    return tuple(jnp.asarray(np.array(t, np.int32)) for t in (fb, bb, first, last, sid))


S5_GB = 8
S5_GW = S5_GB * S5_STATE
S5_ROWS = N_CTX
assert LAT_L == S5_ROWS
S5_LAT_SEGS = 8


def _s5_tables(lam_re, lam_im, log_dt, b_re, b_im, c_re, c_im):
    dt = jnp.exp(log_dt)[..., None]
    lr, li = lam_re * dt, lam_im * dt
    a_re, a_im = jnp.exp(lr) * jnp.cos(li), jnp.exp(lr) * jnp.sin(li)
    den = lam_re * lam_re + lam_im * lam_im
    fr = ((a_re - 1.0) * lam_re + a_im * lam_im) / den
    fi = (a_im * lam_re - (a_re - 1.0) * lam_im) / den
    bb_re = fr[..., None] * b_re - fi[..., None] * b_im
    bb_im = fr[..., None] * b_im + fi[..., None] * b_re
    eye = jnp.eye(S5_GB, dtype=F32)

    def bdiag_in(t):
        t = t.reshape(2, S5_GROUPS // S5_GB, S5_GB, S5_STATE, S5_GROUP)
        return jnp.einsum('dbgpc,gh->dbgchp', t, eye).reshape(2, S5_GROUPS // S5_GB, 128, 512)

    def bdiag_out(t):
        t = t.reshape(2, S5_GROUPS // S5_GB, S5_GB, S5_GROUP, S5_STATE)
        return jnp.einsum('dbgcp,gh->dbgphc', t, eye).reshape(2, S5_GROUPS // S5_GB, 512, 128)

    wb = jnp.stack([bdiag_in(bb_re), bdiag_in(bb_im)], axis=1).astype(BF16)
    wc = jnp.stack([bdiag_out(c_re), bdiag_out(c_im)], axis=1).astype(BF16)

    abar = jnp.stack([a_re.reshape(2, S5_N), a_im.reshape(2, S5_N)], axis=1).reshape(4, S5_N)
    return wb, wc, abar


def _hmod_kernel(x_ref, mod_ref, g_ref, o_ref):
    o_ref[...] = _modulate(x_ref[...], g_ref[0], mod_ref[0]).astype(BF16)


def _hmod(x, mods, ng, layer):
    return pl.pallas_call(
        _hmod_kernel,
        out_shape=jax.ShapeDtypeStruct((N_TOK, D), BF16),
        grid=(N_TOK // TM,),
        in_specs=[pl.BlockSpec((TM, D), lambda i: (i, 0)),
                  pl.BlockSpec((1, 1, 3 * D), lambda i: (layer * 3 + _seg_of_tile(i, TM), 0, 1)),
                  pl.BlockSpec((1, 1, D), lambda i: (layer * 6 + 2, 0, 0))],
        out_specs=pl.BlockSpec((TM, D), lambda i: (i, 0)),
        compiler_params=_cparams(("parallel",)),
        name="hmod",
    )(x, mods, ng)


def _s5_scan(bur, bui, ar, ai, x0, n_seq, steps, d, store):
    def body(k, carry):
        xr, xi = carry
        t = k if d == 0 else steps - 1 - k
        r0 = pl.multiple_of(t * n_seq, n_seq)
        nr = ar * xr - ai * xi + bur[pl.ds(r0, n_seq), :]
        ni = ar * xi + ai * xr + bui[pl.ds(r0, n_seq), :]
        if store:
            bur[pl.ds(r0, n_seq), :] = nr
            bui[pl.ds(r0, n_seq), :] = ni
        return nr, ni

    return lax.fori_loop(0, steps, body, x0, unroll=8)


def _s5_seg_carries(er, ei, h0r, h0i, ar, ai, steps, d):
    pr, pi = ar, ai
    for _ in range(int(math.log2(steps))):
        pr, pi = pr * pr - pi * pi, 2.0 * pr * pi
    row = lax.broadcasted_iota(jnp.int32, er.shape, 0)
    n = S5_LAT_SEGS
    if d == 0:
        cr = jnp.where(row == 0, h0r, pltpu.roll(er, 1, 0))
        ci = jnp.where(row == 0, h0i, pltpu.roll(ei, 1, 0))
    else:
        cr = jnp.where(row == n - 1, h0r, pltpu.roll(er, n - 1, 0))
        ci = jnp.where(row == n - 1, h0i, pltpu.roll(ei, n - 1, 0))
    s = 1
    while s < n:
        keep = (row >= s) if d == 0 else (row + s <= n - 1)
        sh = s if d == 0 else n - s
        sr = jnp.where(keep, pltpu.roll(cr, sh, 0), 0.0)
        si = jnp.where(keep, pltpu.roll(ci, sh, 0), 0.0)
        cr, ci = cr + pr * sr - pi * si, ci + pr * si + pi * sr
        pr, pi = pr * pr - pi * pi, 2.0 * pr * pi
        s *= 2
    return cr, ci


def _s5_kernel(*refs, n_seq):
    if n_seq == S5_LAT_SEGS:
        u_ref, wb_ref, wc_ref, a_ref, h0_ref, y_ref, bur, bui = refs
    else:
        u_ref, wb_ref, wc_ref, a_ref, y_ref, fin_ref, bur, bui = refs
    steps = S5_ROWS // n_seq
    u = u_ref[0]
    y = None
    for d in range(2):
        bur[...] = _dot(u, wb_ref[d, 0, 0])
        bui[...] = _dot(u, wb_ref[d, 1, 0])
        ar = jnp.broadcast_to(a_ref[2 * d:2 * d + 1, :], (n_seq, S5_GW))
        ai = jnp.broadcast_to(a_ref[2 * d + 1:2 * d + 2, :], (n_seq, S5_GW))
        zero = jnp.zeros((n_seq, S5_GW), F32)
        if n_seq == S5_LAT_SEGS:
            er, ei = _s5_scan(bur, bui, ar, ai, (zero, zero), n_seq, steps, d, store=False)
            x0 = _s5_seg_carries(er, ei, h0_ref[0, 2 * d:2 * d + 1, :], h0_ref[0, 2 * d + 1:2 * d + 2, :],
                                 ar, ai, steps, d)
        else:
            x0 = (zero, zero)
        xr, xi = _s5_scan(bur, bui, ar, ai, x0, n_seq, steps, d, store=True)
        if n_seq != S5_LAT_SEGS:
            fin_ref[2 * d] = xr
            fin_ref[2 * d + 1] = xi
        yd = _dot(bur[...].astype(BF16), wc_ref[d, 0, 0]) - _dot(bui[...].astype(BF16), wc_ref[d, 1, 0])
        y = yd if y is None else y + yd
    y_ref[0] = y


def _s5(u_tm, wb, wc, abar, h0=None):
    n_sets = u_tm.shape[0]
    lat = h0 is not None
    n_seq = S5_LAT_SEGS if lat else CTX_B
    in_specs = [pl.BlockSpec((1, S5_ROWS, 128), lambda s, g: (s, 0, g)),
                pl.BlockSpec((2, 2, 1, 128, S5_GW), lambda s, g: (0, 0, g, 0, 0)),
                pl.BlockSpec((2, 2, 1, S5_GW, 128), lambda s, g: (0, 0, g, 0, 0)),
                pl.BlockSpec((4, S5_GW), lambda s, g: (0, g))]
    args = [u_tm, wb, wc, abar]
    out_shape = [jax.ShapeDtypeStruct((n_sets, S5_ROWS, D), F32)]
    out_specs = [pl.BlockSpec((1, S5_ROWS, 128), lambda s, g: (s, 0, g))]
    if lat:
        in_specs.append(pl.BlockSpec((1, 4, S5_GW), lambda s, g: (s, 0, g)))
        args.append(h0)
    else:
        out_shape.append(jax.ShapeDtypeStruct((4, CTX_B, S5_N), F32))
        out_specs.append(pl.BlockSpec((4, CTX_B, S5_GW), lambda s, g: (0, 0, g)))
    return pl.pallas_call(
        functools.partial(_s5_kernel, n_seq=n_seq),
        out_shape=out_shape,
        grid=(n_sets, S5_GROUPS // S5_GB),
        in_specs=in_specs,
        out_specs=out_specs,
        scratch_shapes=[pltpu.VMEM((S5_ROWS, S5_GW), F32), pltpu.VMEM((S5_ROWS, S5_GW), F32)],
        compiler_params=_cparams(("parallel", "parallel")),
        name="s5_lat" if lat else "s5_ctx",
    )(*args)


def _dn_prep_kernel(blk_ref, first_ref, last_ref, x_ref, prev_ref, next_ref, ba_ref, cw_ref, alog_ref, dtb_ref,
                    q_ref, k_ref, v_ref, g_ref):
    i = pl.program_id(0)
    nqkv = 2 * DN_NQK + DN_NV
    x = x_ref[...]
    pv = jnp.where(first_ref[i] == 1, 0.0, prev_ref[...])
    nx = jnp.where(last_ref[i] == 1, 0.0, next_ref[...])
    ext = jnp.concatenate([pv, x, nx], axis=0)
    n_ext = SEQ_BLK + 16
    acc = None
    for j in range(DN_CONV):
        off = DN_CONV // 2 - j
        e = ext if off == 0 else pltpu.roll(ext, off % n_ext, 0)
        term = cw_ref[j:j + 1, :] * e[8:8 + SEQ_BLK]
        acc = term if acc is None else acc + term
    a = _silu(acc)
    for h in range(DN_QK):
        cs = slice(h * DN_HD, (h + 1) * DN_HD)
        qh = a[:, cs]
        q_ref[:, cs] = qh * lax.rsqrt(jnp.sum(qh * qh, axis=-1, keepdims=True) + EPS) * (DN_HD ** -0.5)
        kh = a[:, DN_NQK + h * DN_HD:DN_NQK + (h + 1) * DN_HD]
        k_ref[:, cs] = kh * lax.rsqrt(jnp.sum(kh * kh, axis=-1, keepdims=True) + EPS)
    v_ref[...] = a[:, 2 * DN_NQK:nqkv]

    ba = ba_ref[...]
    lane = lax.broadcasted_iota(jnp.int32, ba.shape, 1)
    row = lax.broadcasted_iota(jnp.int32, ba.shape, 0) % DN_CHUNK
    is_g = ((lane % 16) >= 8) & (lane < 32)
    z = ba + dtb_ref[...]
    softplus = jnp.maximum(z, 0.0) + jnp.log1p(jnp.exp(-jnp.abs(z)))
    val = jnp.where(is_g, -jnp.exp(alog_ref[...]) * softplus, jax.nn.sigmoid(ba))
    cf = val
    cr = val
    s = 1
    while s < DN_CHUNK:
        cf = cf + jnp.where(row >= s, pltpu.roll(cf, s, 0), 0.0)
        cr = cr + jnp.where(row < DN_CHUNK - s, pltpu.roll(cr, SEQ_BLK - s, 0), 0.0)
        s *= 2
    g_ref[...] = jnp.where(lane < 32, jnp.where(is_g, jnp.where(lane < 16, cf, cr), val), 0.0)


def _dn_prep(proj, conv_w, alog_row, dtb_row, tables):
    fb, _, first, last, _ = tables
    nqkv = 2 * DN_NQK + DN_NV
    per = SEQ_BLK // 8
    n8 = N_TOK // 8
    grid_spec = pltpu.PrefetchScalarGridSpec(
        num_scalar_prefetch=3,
        grid=(fb.shape[0],),
        in_specs=[pl.BlockSpec((SEQ_BLK, nqkv), lambda i, blk, *_: (blk[i], 0)),
                  pl.BlockSpec((8, nqkv), lambda i, blk, *_: (jnp.maximum(blk[i] * per - 1, 0), 0)),
                  pl.BlockSpec((8, nqkv), lambda i, blk, *_: (jnp.minimum(blk[i] * per + per, n8 - 1), 0)),
                  pl.BlockSpec((SEQ_BLK, 128), lambda i, blk, *_: (blk[i], (nqkv + DN_NV) // 128)),
                  pl.BlockSpec((DN_CONV, nqkv), lambda i, *_: (0, 0)),
                  pl.BlockSpec((1, 128), lambda i, *_: (0, 0)),
                  pl.BlockSpec((1, 128), lambda i, *_: (0, 0))],
        out_specs=[pl.BlockSpec((SEQ_BLK, DN_NQK), lambda i, blk, *_: (blk[i], 0)),
                   pl.BlockSpec((SEQ_BLK, DN_NQK), lambda i, blk, *_: (blk[i], 0)),
                   pl.BlockSpec((SEQ_BLK, DN_NV), lambda i, blk, *_: (blk[i], 0)),
                   pl.BlockSpec((SEQ_BLK, 128), lambda i, blk, *_: (blk[i], 0))])
    return pl.pallas_call(
        _dn_prep_kernel,
        out_shape=[jax.ShapeDtypeStruct((N_TOK, DN_NQK), F32), jax.ShapeDtypeStruct((N_TOK, DN_NQK), F32),
                   jax.ShapeDtypeStruct((N_TOK, DN_NV), F32), jax.ShapeDtypeStruct((N_TOK, 128), F32)],
        grid_spec=grid_spec,
        compiler_params=_cparams(("arbitrary",)),
        name="dn_prep",
    )(fb, first, last, proj, proj, proj, proj, conv_w, alog_row, dtb_row)


DN_NCH = SEQ_BLK // DN_CHUNK


DN_LEVELS = int(math.log2(DN_CHUNK))


def _dn_setup(d, lane_beta, q, k, kk, qk, v, gates):
    n = SEQ_BLK
    lane = lax.broadcasted_iota(jnp.int32, gates.shape, 1)
    beta = jnp.sum(jnp.where(lane == lane_beta, gates, 0.0), axis=1, keepdims=True)
    gcol = jnp.sum(jnp.where(lane == lane_beta + 8, gates, 0.0), axis=1, keepdims=True)
    ri = lax.broadcasted_iota(jnp.int32, (n, n), 0)
    ci = lax.broadcasted_iota(jnp.int32, (n, n), 1)
    eye = ri == ci
    same = (ri // DN_CHUNK) == (ci // DN_CHUNK)
    grow = jnp.sum(jnp.where(eye, gcol, 0.0), axis=0, keepdims=True)
    if d == 0:
        lower, strict = same & (ri >= ci), same & (ri > ci)
        last_of = (ri // DN_CHUNK) * DN_CHUNK + DN_CHUNK - 1
    else:
        lower, strict = same & (ri <= ci), same & (ri < ci)
        last_of = (ri // DN_CHUNK) * DN_CHUNK
    decay = jnp.exp(jnp.where(lower, gcol - grow, NEG))
    g_last = jnp.sum(jnp.where(ci == last_of, grow, 0.0), axis=1, keepdims=True)
    return dict(
        d=d,
        lmat=jnp.where(strict, beta * kk * decay, 0.0),
        rhs16=jnp.concatenate([v * beta, k * (beta * jnp.exp(gcol))], axis=1).astype(BF16),
        aqk16=jnp.where(lower, qk * decay, 0.0).astype(BF16),
        qe16=(q * jnp.exp(gcol)).astype(BF16),
        kd16=(k * jnp.exp(g_last - gcol)).astype(BF16),
        eg=jnp.exp(g_last))


def _dn_solve_all(chains):
    n = SEQ_BLK
    ri = lax.broadcasted_iota(jnp.int32, (n, n), 0)
    ci = lax.broadcasted_iota(jnp.int32, (n, n), 1)
    diff = ri ^ ci
    level = sum((diff >= (1 << b)).astype(jnp.int32) for b in range(DN_LEVELS + 1))
    ts = [jnp.where(ri == ci, 1.0, 0.0) - jnp.where(level == 1, c["lmat"], 0.0) for c in chains]
    for lv in range(2, DN_LEVELS + 1):
        t16 = [t.astype(BF16) for t in ts]
        xs = [_dot(jnp.where(level == lv, c["lmat"], 0.0).astype(BF16), t) for c, t in zip(chains, t16)]
        ts = [t - _dot(th, x.astype(BF16)) for t, th, x in zip(ts, t16, xs)]
    return [_dot(t.astype(BF16), c["rhs16"]) for c, t in zip(chains, ts)]


def _dn_scan_all(chains, sols, states):
    us = [s[:, :DN_HD] for s in sols]
    w16 = [s[:, DN_HD:].astype(BF16) for s in sols]
    outs = [[None] * DN_NCH for _ in chains]
    for step in range(DN_NCH):
        idx = [step if c["d"] == 0 else DN_NCH - 1 - step for c in chains]
        rows = [slice(i * DN_CHUNK, (i + 1) * DN_CHUNK) for i in idx]
        s16 = [s.astype(BF16) for s in states]
        vn16 = [(u[r] - _dot(w[r], s)).astype(BF16) for u, w, r, s in zip(us, w16, rows, s16)]
        for n_, (c, r, s, vn) in enumerate(zip(chains, rows, s16, vn16)):
            outs[n_][idx[n_]] = _dot(c["qe16"][r], s) + _dot(c["aqk16"][r, r], vn)
        states = [s * c["eg"][r.start:r.start + 1] + _dot_tn(c["kd16"][r], vn)
                  for s, c, r, vn in zip(states, chains, rows, vn16)]
    return [jnp.concatenate(o, axis=0) for o in outs], states


def _dn_kernel(fb_ref, bb_ref, first_ref, last_ref, sid_ref,
               qf_ref, kf_ref, vf_ref, gf_ref, qb_ref, kb_ref, vb_ref, gb_ref, s0_ref,
               of_ref, ob_ref, so_ref, s_ref):
    j = pl.program_id(0)
    i = pl.program_id(1)
    is_first = first_ref[i] == 1
    is_lat = sid_ref[i] >= CTX_B

    @pl.when(is_first & is_lat)
    def _():
        s_ref[...] = s0_ref[0]

    @pl.when(is_first & jnp.logical_not(is_lat))
    def _():
        s_ref[...] = jnp.zeros_like(s_ref)

    chains = []
    for d in range(2):
        q_ref, k_ref, v_ref, g_ref = (qf_ref, kf_ref, vf_ref, gf_ref) if d == 0 else (qb_ref, kb_ref, vb_ref, gb_ref)
        q, k, gates = q_ref[...], k_ref[...], g_ref[...]
        k16 = k.astype(BF16)
        kk = _dot_nt(k16, k16)
        qk = _dot_nt(q.astype(BF16), k16)
        for e in range(DN_REP):
            chains.append(_dn_setup(d, 16 * d + DN_REP * j + e, q, k, kk, qk,
                                    v_ref[:, e * DN_HD:(e + 1) * DN_HD], gates))
    sols = _dn_solve_all(chains)
    outs, states = _dn_scan_all(chains, sols, [s_ref[d, e] for d in range(2) for e in range(DN_REP)])
    for n_, (o, s_new) in enumerate(zip(outs, states)):
        d, e = divmod(n_, DN_REP)
        (of_ref if d == 0 else ob_ref)[:, e * DN_HD:(e + 1) * DN_HD] = o
        s_ref[d, e] = s_new
    so_ref[0] = s_ref[...]


def _dn(qn, kn, vv, gates, state, tables):
    n_steps = tables[0].shape[0]
    n_seq = CTX_B + LAT_B

    def fwd(col):
        return lambda j, i, fb, *_: (fb[i], col(j))

    def bwd(col):
        return lambda j, i, fb, bb, *_: (bb[i], col(j))

    head = lambda j: j
    gate = lambda j: 0

    def st_in(j, i, fb, bb, first, last, sid):
        return (jnp.maximum(sid[i] - CTX_B, 0), 0, j, 0, 0)

    def st_out(j, i, fb, bb, first, last, sid):
        return (sid[i], 0, j, 0, 0)

    blk = lambda w: (SEQ_BLK, w)
    vw = DN_REP * DN_HD
    grid_spec = pltpu.PrefetchScalarGridSpec(
        num_scalar_prefetch=5,
        grid=(DN_QK, n_steps),
        in_specs=[pl.BlockSpec(blk(DN_HD), fwd(head)), pl.BlockSpec(blk(DN_HD), fwd(head)),
                  pl.BlockSpec(blk(vw), fwd(head)), pl.BlockSpec(blk(128), fwd(gate)),
                  pl.BlockSpec(blk(DN_HD), bwd(head)), pl.BlockSpec(blk(DN_HD), bwd(head)),
                  pl.BlockSpec(blk(vw), bwd(head)), pl.BlockSpec(blk(128), bwd(gate)),
                  pl.BlockSpec((1, 2, DN_REP, DN_HD, DN_HD), st_in)],
        out_specs=[pl.BlockSpec(blk(vw), fwd(head)), pl.BlockSpec(blk(vw), bwd(head)),
                   pl.BlockSpec((1, 2, DN_REP, DN_HD, DN_HD), st_out)],
        scratch_shapes=[pltpu.VMEM((2, DN_REP, DN_HD, DN_HD), F32)])
    return pl.pallas_call(
        _dn_kernel,
        out_shape=[jax.ShapeDtypeStruct((N_TOK, DN_NV), F32), jax.ShapeDtypeStruct((N_TOK, DN_NV), F32),
                   jax.ShapeDtypeStruct((n_seq, 2, DN_V, DN_HD, DN_HD), F32)],
        grid_spec=grid_spec,
        compiler_params=_cparams(("arbitrary", "arbitrary")),
        name="dn",
    )(*tables, qn, kn, vv, gates, qn, kn, vv, gates, state)


def kernel(x_prompt, x_sample, cache_attn_k, cache_attn_v, state_s5_re, state_s5_im, cache_na_k, cache_na_v,
           state_dn, c, c_ctx, norm_g, w_ada, b_ada, ffn_w_gu, ffn_w_d, a_w_qkv, a_w_o, a_sink,
           s5_lam_re, s5_lam_im, s5_log_dt, s5_b_re, s5_b_im, s5_c_re, s5_c_im, s5_d, s5_w_glu,
           na_w_qkv, na_w_o, na_rpb, dn_w_in, dn_conv_w, dn_w_ba, dn_a_log, dn_dt_bias, dn_out_g, dn_w_o):
    depth = w_ada.shape[0]
    cond8 = jnp.concatenate([c_ctx[None, :], c, jnp.zeros((8 - 1 - LAT_B, D), F32)], axis=0)
    mods = _ada(cond8, w_ada, b_ada)[:, :1 + LAT_B].reshape(depth * (1 + LAT_B), 1, 9 * D)
    ng = norm_g.reshape(depth * 6, 1, D)
    wgu = ffn_w_gu.astype(BF16)
    wd = ffn_w_d.astype(BF16)
    tables = _seq_tables()

    x = _ffn([x_prompt.reshape(N_CTX, D), x_sample.reshape(N_LAT, D)], mods, ng, wgu, wd, 0, 0, head="split")
    qw, kw = A_HEADS * HEAD, A_KV * HEAD
    qkv, k_ctx, v_ctx = _proj(x, mods, ng, a_w_qkv[0].astype(BF16), 0,
                              kv_cols=((qw, qw + kw), (qw + kw, qw + 2 * kw)))
    cos, sin = _rope_tables()
    o_ctx = _attn_ctx(qkv, A_HEADS, A_KV, a_sink[0])
    o_lat = _attn_a_lat(qkv, a_sink[0], cache_attn_k[:, 0].reshape(LAT_B, PAST, kw),
                        cache_attn_v[:, 0].reshape(LAT_B, PAST, kw), cos, sin)
    new_attn_k = k_ctx.reshape(CTX_B, 1, CTX_L, A_KV, HEAD)
    new_attn_v = v_ctx.reshape(CTX_B, 1, CTX_L, A_KV, HEAD)
    x = _ffn([o_ctx, o_lat, a_w_o[0].astype(BF16), x], mods, ng, wgu, wd, 0, 1, head="pair")

    x = _ffn([x], mods, ng, wgu, wd, 1, 0)
    wb, wc, abar = _s5_tables(s5_lam_re[0], s5_lam_im[0], s5_log_dt[0], s5_b_re[0], s5_b_im[0],
                              s5_c_re[0], s5_c_im[0])
    hb = _hmod(x, mods, ng, 1)
    seg = LAT_L // S5_LAT_SEGS
    u_ctx = hb[:N_CTX].reshape(CTX_B, CTX_L, D).transpose(1, 0, 2).reshape(1, N_CTX, D)
    u_lat = hb[N_CTX:].reshape(LAT_B, S5_LAT_SEGS, seg, D).transpose(0, 2, 1, 3).reshape(LAT_B, LAT_L, D)
    h0 = jnp.stack([state_s5_re[:, 0], state_s5_im[:, 0]], axis=2).reshape(LAT_B, 4, S5_N)
    y_ctx, fin = _s5(u_ctx, wb, wc, abar)
    y_lat, = _s5(u_lat, wb, wc, abar, h0)
    y_ctx = y_ctx.reshape(CTX_L, CTX_B, D).transpose(1, 0, 2).reshape(N_CTX, D)
    y_lat = y_lat.reshape(LAT_B, seg, S5_LAT_SEGS, D).transpose(0, 2, 1, 3).reshape(N_LAT, D)
    fin = fin.reshape(2, 2, CTX_B, S5_GROUPS, S5_STATE)
    new_s5_re = jnp.transpose(fin[:, 0], (1, 0, 2, 3))[:, None]
    new_s5_im = jnp.transpose(fin[:, 1], (1, 0, 2, 3))[:, None]
    x = _ffn([y_ctx, y_lat, s5_d, s5_w_glu[0].astype(BF16), x], mods, ng, wgu, wd, 1, 1, head="s5")

    x = _ffn([x], mods, ng, wgu, wd, 2, 0)
    cw = C_HEADS * HEAD
    qkv, k_ctx, v_ctx = _proj(x, mods, ng, na_w_qkv[0].astype(BF16), 2, kv_cols=((cw, 2 * cw), (2 * cw, 3 * cw)))
    o_ctx = _attn_ctx(qkv, C_HEADS, C_HEADS)
    o_lat = _attn_c_lat(qkv, cache_na_k[:, 0].reshape(LAT_B, PAST, cw),
                        cache_na_v[:, 0].reshape(LAT_B, PAST, cw), *_na_tables(na_rpb[0]))
    new_na_k = k_ctx.reshape(CTX_B, 1, CTX_L, C_HEADS, HEAD)
    new_na_v = v_ctx.reshape(CTX_B, 1, CTX_L, C_HEADS, HEAD)
    x = _ffn([o_ctx, o_lat, na_w_o[0].astype(BF16), x], mods, ng, wgu, wd, 2, 1, head="pair")

    x = _ffn([x], mods, ng, wgu, wd, 3, 0)
    w_all = jnp.concatenate([dn_w_in[0], dn_w_ba[0, 0], dn_w_ba[0, 1],
                             jnp.zeros((D, 128 - 4 * DN_V), F32)], axis=1).astype(BF16)
    proj, = _proj(x, mods, ng, w_all, 3)
    pad8 = jnp.zeros((DN_V,), F32)
    gate_row = lambda t: jnp.concatenate([pad8, t[0], pad8, t[1], jnp.zeros((128 - 4 * DN_V,), F32)])[None, :]
    qn, kn, vv, gates = _dn_prep(proj, dn_conv_w[0], gate_row(dn_a_log[0]), gate_row(dn_dt_bias[0]), tables)
    of, ob, fin_dn = _dn(qn, kn, vv, gates, state_dn[:, 0], tables)
    new_dn = fin_dn[:CTX_B][:, None]
    y, z = _ffn([of, ob, proj, dn_out_g, dn_w_o[0].astype(BF16), x], mods, ng, wgu, wd, 3, 1,
                head="dn", split_out=True)
    y = y.reshape(CTX_B, CTX_L, D)
    z = z.reshape(LAT_B, LAT_L, D)
    return (y, z, new_attn_k, new_attn_v, new_s5_re, new_s5_im, new_na_k, new_na_v, new_dn)
```

```python
import functools
import math

import numpy as np
import jax
import jax.numpy as jnp
from jax import lax
from jax.experimental import pallas as pl
from jax.experimental.pallas import tpu as pltpu

F32 = jnp.float32
BF16 = jnp.bfloat16

D = 1024
CTX_B, CTX_L = 16, 256
LAT_B, LAT_L = 2, 4096
N_CTX = CTX_B * CTX_L
N_LAT = LAT_B * LAT_L
N_TOK = N_CTX + N_LAT
PAST = 512
D_FF = 2816
EPS = 1e-6
NEG = -1e30
GRID_W = 64
HEAD = 64
A_HEADS, A_KV = 16, 4
A_WIN = 128
C_HEADS = 16
NA_ROWS, NA_COLS = 8, 16
ROPE_BASE = 10000.0
S5_GROUPS, S5_GROUP, S5_STATE = 64, 16, 64
S5_N = S5_GROUPS * S5_STATE
DN_QK, DN_V, DN_HD, DN_CONV, DN_CHUNK = 4, 8, 128, 5, 64
DN_REP = DN_V // DN_QK
DN_NQK = DN_QK * DN_HD
DN_NV = DN_V * DN_HD
DN_PROJ = 2 * DN_NQK + 2 * DN_NV + 128

TM = 512
SEQ_BLK = 256
VMEM_MB = 56
ADA_TN = 1152


def _cparams(sem, mb=VMEM_MB):
    return pltpu.CompilerParams(dimension_semantics=sem, vmem_limit_bytes=mb * 1024 * 1024)


def _resident(block, index_map):
    return pl.BlockSpec(block, index_map, pipeline_mode=pl.Buffered(1))


def _seg_of_tile(i, tm):
    nct = N_CTX // tm
    return jnp.where(i < nct, 0, 1 + (i - nct) // (LAT_L // tm))


def _dot(a, b):
    return jnp.dot(a, b, preferred_element_type=F32)


def _dot_nt(a, b):
    return lax.dot_general(a, b, (((1,), (1,)), ((), ())), preferred_element_type=F32)


def _dot_tn(a, b):
    return lax.dot_general(a, b, (((0,), (0,)), ((), ())), preferred_element_type=F32)


def _silu(x):
    return x * jax.nn.sigmoid(x)


def _modulate(x, g, m):
    xn = x * lax.rsqrt(jnp.mean(x * x, axis=-1, keepdims=True) + EPS) * g
    return xn * (1.0 + m[:, D:2 * D]) + m[:, :D]


def _residual(x, y, g, m, weight):
    yn = y * lax.rsqrt(jnp.mean(y * y, axis=-1, keepdims=True) + EPS) * g
    return x + weight * m[:, 2 * D:] * yn


def _ada_kernel(cond_ref, w_ref, b_ref, o_ref):
    s = _silu(cond_ref[...]).astype(BF16)
    o_ref[0] = _dot(s, w_ref[0].astype(BF16)) + b_ref[0]


def _ada(cond8, w_ada, b_ada):
    depth, _, n9 = w_ada.shape
    return pl.pallas_call(
        _ada_kernel,
        out_shape=jax.ShapeDtypeStruct((depth, 8, n9), F32),
        grid=(depth, n9 // ADA_TN),
        in_specs=[pl.BlockSpec((8, D), lambda l, j: (0, 0)),
                  pl.BlockSpec((1, D, ADA_TN), lambda l, j: (l, 0, j)),
                  pl.BlockSpec((1, 1, ADA_TN), lambda l, j: (l, 0, j))],
        out_specs=pl.BlockSpec((1, 8, ADA_TN), lambda l, j: (l, 0, j)),
        compiler_params=_cparams(("parallel", "parallel")),
        name="ada",
    )(cond8, w_ada, b_ada.reshape(depth, 1, n9))


FF_CW = 256


def _ffn_input(head, refs, is_ctx):
    if head == "plain":
        return refs[0][...]
    if head == "split":
        return jnp.where(is_ctx, refs[0][...], refs[1][...])
    x_ref, mod_ref, gpost_ref = refs[-3:]
    x, m = x_ref[...], mod_ref[0]
    if head == "pair":
        actx_ref, alat_ref, w_ref = refs[:3]
        a = jnp.where(is_ctx, actx_ref[...], alat_ref[...]).astype(BF16)
        y = _dot(a, w_ref[...])
    elif head == "s5":
        yc_ref, yl_ref, dsk_ref, gpre_ref, w_ref = refs[:5]
        y = jnp.where(is_ctx, yc_ref[...], yl_ref[...]) + dsk_ref[...] * _modulate(x, gpre_ref[0], m)
        t = _dot(jax.nn.gelu(y).astype(BF16), w_ref[...])
        y = t[:, :D] * jax.nn.sigmoid(t[:, D:])
    else:
        of_ref, ob_ref, z_ref, og_ref, w_ref = refs[:5]
        o = of_ref[...] + ob_ref[...]
        og = og_ref[...]
        heads = []
        for h in range(DN_V):
            cs = slice(h * DN_HD, (h + 1) * DN_HD)
            oh = o[:, cs]
            on = oh * lax.rsqrt(jnp.mean(oh * oh, axis=-1, keepdims=True) + EPS) * og
            heads.append((on * _silu(z_ref[:, cs])).astype(BF16))
        y = _dot(jnp.concatenate(heads, axis=1), w_ref[...])
    return _residual(x, y, gpost_ref[0], m, 1.0)


def _ffn_kernel(*refs, head, n_head, n_out):
    mod_ref, gpre_ref, gpost_ref, wgu_ref, wd_ref = refs[n_head:n_head + 5]
    o_refs, act_ref = refs[n_head + 5:n_head + 5 + n_out], refs[-1]
    is_ctx = pl.program_id(0) < N_CTX // TM
    x = _ffn_input(head, refs[:n_head], is_ctx)
    m = mod_ref[0]
    h = _modulate(x, gpre_ref[0], m).astype(BF16)
    for c in range(D_FF // FF_CW):
        g = _dot(h, wgu_ref[:, c * FF_CW:(c + 1) * FF_CW])
        u = _dot(h, wgu_ref[:, D_FF + c * FF_CW:D_FF + (c + 1) * FF_CW])
        act_ref[:, c * FF_CW:(c + 1) * FF_CW] = (_silu(g) * u).astype(BF16)
    y = _dot(act_ref[...], wd_ref[...])
    out = _residual(x, y, gpost_ref[0], m, 0.5)
    if n_out == 1:
        o_refs[0][...] = out
    else:
        @pl.when(is_ctx)
        def _():
            o_refs[0][...] = out

        @pl.when(jnp.logical_not(is_ctx))
        def _():
            o_refs[1][...] = out


def _ffn(xs, mods, ng, wgu, wd, layer, j, head="plain", split_out=False):
    s = 2 * j
    nct = N_CTX // TM
    ctx_spec = pl.BlockSpec((TM, D), lambda i: (jnp.minimum(i, nct - 1), 0))
    lat_spec = pl.BlockSpec((TM, D), lambda i: (jnp.maximum(i - nct, 0), 0))
    all_spec = pl.BlockSpec((TM, D), lambda i: (i, 0))
    row_spec = lambda k: pl.BlockSpec((1, 1, D), lambda i: (layer * 6 + k, 0, 0))
    mod_spec = lambda sub: pl.BlockSpec((1, 1, 3 * D), lambda i: (layer * 3 + _seg_of_tile(i, TM), 0, sub))
    whole = lambda a: _resident(a.shape, lambda i: (0,) * a.ndim)
    if head == "plain":
        head_specs = [all_spec]
    elif head == "split":
        head_specs = [ctx_spec, lat_spec]
    else:
        if head == "pair":
            head_specs = [ctx_spec, lat_spec, whole(xs[2])]
        elif head == "s5":
            head_specs = [ctx_spec, lat_spec, pl.BlockSpec((1, D), lambda i: (0, 0)), row_spec(2), whole(xs[3])]
            xs = xs[:3] + [ng] + xs[3:]
        else:
            zblk = (2 * DN_NQK + DN_NV) // DN_NV
            head_specs = [all_spec, all_spec, pl.BlockSpec((TM, DN_NV), lambda i: (i, zblk)),
                          pl.BlockSpec((1, DN_HD), lambda i: (0, 0)), whole(xs[4])]
        head_specs += [all_spec, mod_spec(1), row_spec(3)]
        xs = xs + [mods, ng]
    if split_out:
        out_shape = [jax.ShapeDtypeStruct((N_CTX, D), F32), jax.ShapeDtypeStruct((N_LAT, D), F32)]
        out_specs = [ctx_spec, lat_spec]
    else:
        out_shape, out_specs = jax.ShapeDtypeStruct((N_TOK, D), F32), all_spec
    return pl.pallas_call(
        functools.partial(_ffn_kernel, head=head, n_head=len(xs), n_out=2 if split_out else 1),
        out_shape=out_shape,
        grid=(N_TOK // TM,),
        in_specs=head_specs
                 + [pl.BlockSpec((1, 1, 3 * D), lambda i: (layer * 3 + _seg_of_tile(i, TM), 0, s)),
                    pl.BlockSpec((1, 1, D), lambda i: (layer * 6 + 2 * s, 0, 0)),
                    pl.BlockSpec((1, 1, D), lambda i: (layer * 6 + 2 * s + 1, 0, 0)),
                    _resident((None, None, D, 2 * D_FF), lambda i: (layer, j, 0, 0)),
                    _resident((None, None, D_FF, D), lambda i: (layer, j, 0, 0))],
        out_specs=out_specs,
        scratch_shapes=[pltpu.VMEM((TM, D_FF), BF16)],
        compiler_params=_cparams(("arbitrary",)),
        name="ffn",
    )(*xs, mods, ng, ng, wgu, wd)


def _proj_kernel(x_ref, mod_ref, g_ref, w_ref, o_ref, *kv_refs, kv_cols):
    h = _modulate(x_ref[...], g_ref[0], mod_ref[0]).astype(BF16)
    p = _dot(h, w_ref[...])
    o_ref[...] = p
    if kv_refs:
        @pl.when(pl.program_id(0) < N_CTX // TM)
        def _():
            for ref, (c0, c1) in zip(kv_refs, kv_cols):
                ref[...] = p[:, c0:c1]


def _proj(x, mods, ng, w, layer, kv_cols=()):
    n = w.shape[1]
    nct = N_CTX // TM
    out_shape = [jax.ShapeDtypeStruct((N_TOK, n), F32)]
    out_specs = [pl.BlockSpec((TM, n), lambda i: (i, 0))]
    for c0, c1 in kv_cols:
        out_shape.append(jax.ShapeDtypeStruct((N_CTX, c1 - c0), F32))
        out_specs.append(pl.BlockSpec((TM, c1 - c0), lambda i: (jnp.minimum(i, nct - 1), 0)))
    return pl.pallas_call(
        functools.partial(_proj_kernel, kv_cols=tuple(kv_cols)),
        out_shape=out_shape,
        grid=(N_TOK // TM,),
        in_specs=[pl.BlockSpec((TM, D), lambda i: (i, 0)),
                  pl.BlockSpec((1, 1, 3 * D), lambda i: (layer * 3 + _seg_of_tile(i, TM), 0, 1)),
                  pl.BlockSpec((1, 1, D), lambda i: (layer * 6 + 2, 0, 0)),
                  _resident((D, n), lambda i: (0, 0))],
        out_specs=out_specs,
        compiler_params=_cparams(("arbitrary",)),
        name="proj",
    )(x, mods, ng, w)


def _softmax_pv(parts, sink):
    m = functools.reduce(jnp.maximum, [jnp.max(s, axis=-1, keepdims=True) for s, _ in parts])
    if sink is not None:
        m = jnp.maximum(m, sink)
    l = None
    o = None
    for s, v in parts:
        p = jnp.exp(s - m)
        ls = jnp.sum(p, axis=-1, keepdims=True)
        os_ = _dot(p.astype(BF16), v)
        l = ls if l is None else l + ls
        o = os_ if o is None else o + os_
    if sink is not None:
        l = l + jnp.exp(sink - m)
    return o / l


def _attn_ctx_kernel(*refs, n_heads, group, has_sink):
    if has_sink:
        sink_ref, q_ref, k_ref, v_ref, o_ref = refs
    else:
        q_ref, k_ref, v_ref, o_ref = refs
    scale = HEAD ** -0.5
    for h in range(n_heads):
        kv = h // group
        q = (q_ref[:, h * HEAD:(h + 1) * HEAD] * scale).astype(BF16)
        k = k_ref[:, kv * HEAD:(kv + 1) * HEAD].astype(BF16)
        v = v_ref[:, kv * HEAD:(kv + 1) * HEAD].astype(BF16)
        s = _dot_nt(q, k)
        o = _softmax_pv([(s, v)], sink_ref[h] if has_sink else None)
        o_ref[:, h * HEAD:(h + 1) * HEAD] = o


def _attn_ctx(qkv, n_heads, n_kv, sink=None):
    qw, kw = n_heads * HEAD, n_kv * HEAD
    kern = functools.partial(_attn_ctx_kernel, n_heads=n_heads, group=n_heads // n_kv,
                             has_sink=sink is not None)
    specs = [pl.BlockSpec((CTX_L, qw), lambda b: (b, 0)),
             pl.BlockSpec((CTX_L, kw), lambda b: (b, qw // kw)),
             pl.BlockSpec((CTX_L, kw), lambda b: (b, qw // kw + 1))]
    args = [qkv, qkv, qkv]
    if sink is not None:
        specs = [pl.BlockSpec(memory_space=pltpu.SMEM)] + specs
        args = [sink] + args
    return pl.pallas_call(
        kern,
        out_shape=jax.ShapeDtypeStruct((N_CTX, qw), F32),
        grid=(CTX_B,),
        in_specs=specs,
        out_specs=pl.BlockSpec((CTX_L, qw), lambda b: (b, 0)),
        compiler_params=_cparams(("parallel",)),
        name="attn_ctx",
    )(*args)


def _rope_tables():
    n = HEAD // 4
    inv = ROPE_BASE ** (-jnp.arange(n, dtype=F32) / n)
    t = jnp.arange(LAT_L)
    ang_r = (t // GRID_W).astype(F32)[:, None] * inv[None, :]
    ang_c = (t % GRID_W).astype(F32)[:, None] * inv[None, :]
    cos = jnp.concatenate([jnp.cos(ang_r), jnp.cos(ang_r), jnp.cos(ang_c), jnp.cos(ang_c)], axis=-1)
    sin = jnp.concatenate([-jnp.sin(ang_r), jnp.sin(ang_r), -jnp.sin(ang_c), jnp.sin(ang_c)], axis=-1)
    return jnp.tile(cos, (1, 2)), jnp.tile(sin, (1, 2))


def _rope(x, cos, sin):
    rows = x.shape[0]
    lane = lax.broadcasted_iota(jnp.int32, (rows, 128), 1)
    first = (lane % 32) < 16
    outs = []
    for cb in range(x.shape[1] // 128):
        xb = x[:, cb * 128:(cb + 1) * 128]
        partner = jnp.where(first, pltpu.roll(xb, 112, 1), pltpu.roll(xb, 16, 1))
        outs.append(xb * cos + partner * sin)
    return jnp.concatenate(outs, axis=1)


A_KWIN = 3 * A_WIN


def _attn_a_lat_kernel(sink_ref, q_ref, k_ref, v_ref, kc_ref, vc_ref, cos_ref, sin_ref, o_ref):
    n = pl.program_id(1)
    nb = LAT_L // A_WIN
    start = pl.multiple_of(jnp.clip(n - 1, 0, nb - 3) * A_WIN, A_WIN)
    q0 = pl.multiple_of(n * A_WIN, A_WIN)
    scale = HEAD ** -0.5
    q = _rope(q_ref[...], cos_ref[pl.ds(q0, A_WIN), :], sin_ref[pl.ds(q0, A_WIN), :]) * scale
    k = _rope(k_ref[pl.ds(start, A_KWIN), :], cos_ref[pl.ds(start, A_KWIN), :],
              sin_ref[pl.ds(start, A_KWIN), :]).astype(BF16)
    v = v_ref[pl.ds(start, A_KWIN), :].astype(BF16)
    kc = kc_ref[0].astype(BF16)
    vc = vc_ref[0].astype(BF16)
    qpos = q0 + lax.broadcasted_iota(jnp.int32, (A_WIN, A_KWIN), 0)
    kpos = start + lax.broadcasted_iota(jnp.int32, (A_WIN, A_KWIN), 1)
    ok = jnp.abs(kpos - qpos) <= A_WIN
    group = A_HEADS // A_KV
    for h in range(A_HEADS):
        kvs = slice((h // group) * HEAD, (h // group + 1) * HEAD)
        qh = q[:, h * HEAD:(h + 1) * HEAD].astype(BF16)
        s_loc = jnp.where(ok, _dot_nt(qh, k[:, kvs]), NEG)
        s_ctx = _dot_nt(qh, kc[:, kvs])
        o = _softmax_pv([(s_loc, v[:, kvs]), (s_ctx, vc[:, kvs])], sink_ref[h])
        o_ref[:, h * HEAD:(h + 1) * HEAD] = o


def _attn_a_lat(qkv, sink, kc, vc, cos, sin):
    qw, kw = A_HEADS * HEAD, A_KV * HEAD
    nb = LAT_L // A_WIN
    return pl.pallas_call(
        _attn_a_lat_kernel,
        out_shape=jax.ShapeDtypeStruct((N_LAT, qw), F32),
        grid=(LAT_B, nb),
        in_specs=[pl.BlockSpec(memory_space=pltpu.SMEM),
                  pl.BlockSpec((A_WIN, qw), lambda b, n: (N_CTX // A_WIN + b * nb + n, 0)),
                  pl.BlockSpec((LAT_L, kw), lambda b, n: (N_CTX // LAT_L + b, qw // kw)),
                  pl.BlockSpec((LAT_L, kw), lambda b, n: (N_CTX // LAT_L + b, qw // kw + 1)),
                  pl.BlockSpec((1, PAST, kw), lambda b, n: (b, 0, 0)),
                  pl.BlockSpec((1, PAST, kw), lambda b, n: (b, 0, 0)),
                  _resident((LAT_L, 128), lambda b, n: (0, 0)),
                  _resident((LAT_L, 128), lambda b, n: (0, 0))],
        out_specs=pl.BlockSpec((A_WIN, qw), lambda b, n: (b * nb + n, 0)),
        compiler_params=_cparams(("parallel", "parallel")),
        name="attn_a_lat",
    )(sink, qkv, qkv, qkv, kc, vc, cos, sin)


NA_QROWS = 8
NA_QTOK = NA_QROWS * GRID_W
NA_KROWS = NA_QROWS + NA_ROWS
NA_KTOK = NA_KROWS * GRID_W
NA_KBLK = (NA_ROWS // 2) * GRID_W
NA_NKB = NA_KTOK // NA_KBLK
NA_HPAIR = 2


def _na_tables(rpb):
    col = jnp.arange(GRID_W)
    dcol = jnp.clip(col[None, :] - col[:, None], 1 - NA_COLS, NA_COLS - 1) + NA_COLS - 1
    cs = jnp.clip(col - NA_COLS // 2, 0, GRID_W - NA_COLS)
    okc = (col[None, :] >= cs[:, None]) & (col[None, :] < cs[:, None] + NA_COLS)
    onehot = (dcol[None] == jnp.arange(2 * NA_COLS - 1)[:, None, None]).astype(F32)
    tiles = jnp.einsum('hdx,xqk->hdqk', rpb, onehot, precision=lax.Precision.HIGHEST)
    bias = jnp.where(okc[None, None], tiles, NEG)
    a = jnp.arange(NA_QROWS)[:, None]
    j = jnp.arange(NA_KROWS)[None, :]
    rows = LAT_L // GRID_W
    masks = []
    for blk in (0, 1, rows // NA_QROWS - 1):
        r = blk * NA_QROWS + a
        key_row = blk * NA_QROWS - NA_ROWS // 2 + j
        rs = jnp.clip(r - NA_ROWS // 2, 0, rows - NA_ROWS)
        valid = (key_row >= rs) & (key_row < rs + NA_ROWS)
        m = jnp.where(valid, 0.0, NEG).astype(F32)
        masks.append(jnp.broadcast_to(m[:, None, :, None], (NA_QROWS, GRID_W, NA_KROWS, GRID_W))
                     .reshape(NA_QTOK, NA_KTOK))
    return bias, jnp.stack(masks)


def _attn_c_lat_kernel(*refs):
    q_ref = refs[0]
    k_refs = refs[1:1 + NA_NKB]
    v_refs = refs[1 + NA_NKB:1 + 2 * NA_NKB]
    kc_ref, vc_ref, tcol_ref, wm_ref, o_ref, bias_ref = refs[1 + 2 * NA_NKB:]
    i = pl.program_id(2)
    nblk = LAT_L // NA_QTOK

    @pl.when((pl.program_id(1) == 0) & (i == 0))
    def _():
        neg = jnp.full((GRID_W, GRID_W), NEG, F32)
        for e in range(NA_HPAIR):
            for a in range(NA_QROWS):
                tiles = []
                for j in range(NA_KROWS):
                    d = j - NA_ROWS // 2 - a + NA_ROWS - 1
                    tiles.append(tcol_ref[e, d] if 0 <= d < 2 * NA_ROWS - 1 else neg)
                bias_ref[e, a * GRID_W:(a + 1) * GRID_W, :] = jnp.concatenate(tiles, axis=1)

    var = jnp.where(i == 0, 0, jnp.where(i == nblk - 1, 2, 1))
    scale = HEAD ** -0.5
    for e in range(NA_HPAIR):
        hs = slice(e * HEAD, (e + 1) * HEAD)
        qh = (q_ref[:, hs] * scale).astype(BF16)
        parts = []
        for m in range(NA_NKB):
            cs = slice(m * NA_KBLK, (m + 1) * NA_KBLK)
            s = _dot_nt(qh, k_refs[m][:, hs].astype(BF16)) + bias_ref[e, :, cs] + wm_ref[var, :, cs]
            parts.append((s, v_refs[m][:, hs].astype(BF16)))
        parts.append((_dot_nt(qh, kc_ref[0, :, hs].astype(BF16)), vc_ref[0, :, hs].astype(BF16)))
        o_ref[:, hs] = _softmax_pv(parts, None)


def _attn_c_lat(qkv, kc, vc, tcol, wmask):
    qw = C_HEADS * HEAD
    hw = NA_HPAIR * HEAD
    nblk = LAT_L // NA_QTOK
    nkb = LAT_L // NA_KBLK
    per = NA_QTOK // NA_KBLK

    def kv_spec(m, col0):
        def idx(hp, b, i):
            return (N_CTX // NA_KBLK + b * nkb + jnp.clip(i * per - 1 + m, 0, nkb - 1), col0 + hp)
        return pl.BlockSpec((NA_KBLK, hw), idx)

    return pl.pallas_call(
        _attn_c_lat_kernel,
        out_shape=jax.ShapeDtypeStruct((N_LAT, qw), F32),
        grid=(C_HEADS // NA_HPAIR, LAT_B, nblk),
        in_specs=[pl.BlockSpec((NA_QTOK, hw), lambda hp, b, i: (N_CTX // NA_QTOK + b * nblk + i, hp))]
                 + [kv_spec(m, qw // hw) for m in range(NA_NKB)]
                 + [kv_spec(m, 2 * qw // hw) for m in range(NA_NKB)]
                 + [pl.BlockSpec((1, PAST, hw), lambda hp, b, i: (b, 0, hp)),
                    pl.BlockSpec((1, PAST, hw), lambda hp, b, i: (b, 0, hp)),
                    pl.BlockSpec((NA_HPAIR, 2 * NA_ROWS - 1, GRID_W, GRID_W), lambda hp, b, i: (hp, 0, 0, 0)),
                    _resident((3, NA_QTOK, NA_KTOK), lambda hp, b, i: (0, 0, 0))],
        out_specs=pl.BlockSpec((NA_QTOK, hw), lambda hp, b, i: (b * nblk + i, hp)),
        scratch_shapes=[pltpu.VMEM((NA_HPAIR, NA_QTOK, NA_KTOK), F32)],
        compiler_params=_cparams(("arbitrary", "arbitrary", "arbitrary")),
        name="attn_c_lat",
    )(qkv, *([qkv] * (2 * NA_NKB)), kc, vc, tcol, wmask)


def _seq_tables():
    fb, bb, first, last, sid = [], [], [], [], []
    base = 0
    for s, length in enumerate([CTX_L] * CTX_B + [LAT_L] * LAT_B):
        n = length // SEQ_BLK
        for c in range(n):
            fb.append(base + c)
            bb.append(base + n - 1 - c)
            first.append(int(c == 0))
            last.append(int(c == n - 1))
            sid.append(s)
        base += n
    return tuple(jnp.asarray(np.array(t, np.int32)) for t in (fb, bb, first, last, sid))


S5_GB = 8
S5_GW = S5_GB * S5_STATE
S5_ROWS = N_CTX
assert LAT_L == S5_ROWS
S5_LAT_SEGS = 8


def _s5_tables(lam_re, lam_im, log_dt, b_re, b_im, c_re, c_im):
    dt = jnp.exp(log_dt)[..., None]
    lr, li = lam_re * dt, lam_im * dt
    a_re, a_im = jnp.exp(lr) * jnp.cos(li), jnp.exp(lr) * jnp.sin(li)
    den = lam_re * lam_re + lam_im * lam_im
    fr = ((a_re - 1.0) * lam_re + a_im * lam_im) / den
    fi = (a_im * lam_re - (a_re - 1.0) * lam_im) / den
    bb_re = fr[..., None] * b_re - fi[..., None] * b_im
    bb_im = fr[..., None] * b_im + fi[..., None] * b_re
    eye = jnp.eye(S5_GB, dtype=F32)

    def bdiag_in(t):
        t = t.reshape(2, S5_GROUPS // S5_GB, S5_GB, S5_STATE, S5_GROUP)
        return jnp.einsum('dbgpc,gh->dbgchp', t, eye).reshape(2, S5_GROUPS // S5_GB, 128, 512)

    def bdiag_out(t):
        t = t.reshape(2, S5_GROUPS // S5_GB, S5_GB, S5_GROUP, S5_STATE)
        return jnp.einsum('dbgcp,gh->dbgphc', t, eye).reshape(2, S5_GROUPS // S5_GB, 512, 128)

    wb = jnp.stack([bdiag_in(bb_re), bdiag_in(bb_im)], axis=1).astype(BF16)
    wc = jnp.stack([bdiag_out(c_re), bdiag_out(c_im)], axis=1).astype(BF16)

    abar = jnp.stack([a_re.reshape(2, S5_N), a_im.reshape(2, S5_N)], axis=1).reshape(4, S5_N)
    return wb, wc, abar


def _hmod_kernel(x_ref, mod_ref, g_ref, o_ref):
    o_ref[...] = _modulate(x_ref[...], g_ref[0], mod_ref[0]).astype(BF16)


def _hmod(x, mods, ng, layer):
    return pl.pallas_call(
        _hmod_kernel,
        out_shape=jax.ShapeDtypeStruct((N_TOK, D), BF16),
        grid=(N_TOK // TM,),
        in_specs=[pl.BlockSpec((TM, D), lambda i: (i, 0)),
                  pl.BlockSpec((1, 1, 3 * D), lambda i: (layer * 3 + _seg_of_tile(i, TM), 0, 1)),
                  pl.BlockSpec((1, 1, D), lambda i: (layer * 6 + 2, 0, 0))],
        out_specs=pl.BlockSpec((TM, D), lambda i: (i, 0)),
        compiler_params=_cparams(("parallel",)),
        name="hmod",
    )(x, mods, ng)


def _s5_scan(bur, bui, ar, ai, x0, n_seq, steps, d, store):
    def body(k, carry):
        xr, xi = carry
        t = k if d == 0 else steps - 1 - k
        r0 = pl.multiple_of(t * n_seq, n_seq)
        nr = ar * xr - ai * xi + bur[pl.ds(r0, n_seq), :]
        ni = ar * xi + ai * xr + bui[pl.ds(r0, n_seq), :]
        if store:
            bur[pl.ds(r0, n_seq), :] = nr
            bui[pl.ds(r0, n_seq), :] = ni
        return nr, ni

    return lax.fori_loop(0, steps, body, x0, unroll=8)


def _s5_seg_carries(er, ei, h0r, h0i, ar, ai, steps, d):
    pr, pi = ar, ai
    for _ in range(int(math.log2(steps))):
        pr, pi = pr * pr - pi * pi, 2.0 * pr * pi
    row = lax.broadcasted_iota(jnp.int32, er.shape, 0)
    n = S5_LAT_SEGS
    if d == 0:
        cr = jnp.where(row == 0, h0r, pltpu.roll(er, 1, 0))
        ci = jnp.where(row == 0, h0i, pltpu.roll(ei, 1, 0))
    else:
        cr = jnp.where(row == n - 1, h0r, pltpu.roll(er, n - 1, 0))
        ci = jnp.where(row == n - 1, h0i, pltpu.roll(ei, n - 1, 0))
    s = 1
    while s < n:
        keep = (row >= s) if d == 0 else (row + s <= n - 1)
        sh = s if d == 0 else n - s
        sr = jnp.where(keep, pltpu.roll(cr, sh, 0), 0.0)
        si = jnp.where(keep, pltpu.roll(ci, sh, 0), 0.0)
        cr, ci = cr + pr * sr - pi * si, ci + pr * si + pi * sr
        pr, pi = pr * pr - pi * pi, 2.0 * pr * pi
        s *= 2
    return cr, ci


def _s5_kernel(*refs, n_seq):
    if n_seq == S5_LAT_SEGS:
        u_ref, wb_ref, wc_ref, a_ref, h0_ref, y_ref, bur, bui = refs
    else:
        u_ref, wb_ref, wc_ref, a_ref, y_ref, fin_ref, bur, bui = refs
    steps = S5_ROWS // n_seq
    u = u_ref[0]
    y = None
    for d in range(2):
        bur[...] = _dot(u, wb_ref[d, 0, 0])
        bui[...] = _dot(u, wb_ref[d, 1, 0])
        ar = jnp.broadcast_to(a_ref[2 * d:2 * d + 1, :], (n_seq, S5_GW))
        ai = jnp.broadcast_to(a_ref[2 * d + 1:2 * d + 2, :], (n_seq, S5_GW))
        zero = jnp.zeros((n_seq, S5_GW), F32)
        if n_seq == S5_LAT_SEGS:
            er, ei = _s5_scan(bur, bui, ar, ai, (zero, zero), n_seq, steps, d, store=False)
            x0 = _s5_seg_carries(er, ei, h0_ref[0, 2 * d:2 * d + 1, :], h0_ref[0, 2 * d + 1:2 * d + 2, :],
                                 ar, ai, steps, d)
        else:
            x0 = (zero, zero)
        xr, xi = _s5_scan(bur, bui, ar, ai, x0, n_seq, steps, d, store=True)
        if n_seq != S5_LAT_SEGS:
            fin_ref[2 * d] = xr
            fin_ref[2 * d + 1] = xi
        yd = _dot(bur[...].astype(BF16), wc_ref[d, 0, 0]) - _dot(bui[...].astype(BF16), wc_ref[d, 1, 0])
        y = yd if y is None else y + yd
    y_ref[0] = y


def _s5(u_tm, wb, wc, abar, h0=None):
    n_sets = u_tm.shape[0]
    lat = h0 is not None
    n_seq = S5_LAT_SEGS if lat else CTX_B
    in_specs = [pl.BlockSpec((1, S5_ROWS, 128), lambda s, g: (s, 0, g)),
                pl.BlockSpec((2, 2, 1, 128, S5_GW), lambda s, g: (0, 0, g, 0, 0)),
                pl.BlockSpec((2, 2, 1, S5_GW, 128), lambda s, g: (0, 0, g, 0, 0)),
                pl.BlockSpec((4, S5_GW), lambda s, g: (0, g))]
    args = [u_tm, wb, wc, abar]
    out_shape = [jax.ShapeDtypeStruct((n_sets, S5_ROWS, D), F32)]
    out_specs = [pl.BlockSpec((1, S5_ROWS, 128), lambda s, g: (s, 0, g))]
    if lat:
        in_specs.append(pl.BlockSpec((1, 4, S5_GW), lambda s, g: (s, 0, g)))
        args.append(h0)
    else:
        out_shape.append(jax.ShapeDtypeStruct((4, CTX_B, S5_N), F32))
        out_specs.append(pl.BlockSpec((4, CTX_B, S5_GW), lambda s, g: (0, 0, g)))
    return pl.pallas_call(
        functools.partial(_s5_kernel, n_seq=n_seq),
        out_shape=out_shape,
        grid=(n_sets, S5_GROUPS // S5_GB),
        in_specs=in_specs,
        out_specs=out_specs,
        scratch_shapes=[pltpu.VMEM((S5_ROWS, S5_GW), F32), pltpu.VMEM((S5_ROWS, S5_GW), F32)],
        compiler_params=_cparams(("parallel", "parallel")),
        name="s5_lat" if lat else "s5_ctx",
    )(*args)


def _dn_prep_kernel(blk_ref, first_ref, last_ref, x_ref, prev_ref, next_ref, ba_ref, cw_ref, alog_ref, dtb_ref,
                    q_ref, k_ref, v_ref, g_ref):
    i = pl.program_id(0)
    nqkv = 2 * DN_NQK + DN_NV
    x = x_ref[...]
    pv = jnp.where(first_ref[i] == 1, 0.0, prev_ref[...])
    nx = jnp.where(last_ref[i] == 1, 0.0, next_ref[...])
    ext = jnp.concatenate([pv, x, nx], axis=0)
    n_ext = SEQ_BLK + 16
    acc = None
    for j in range(DN_CONV):
        off = DN_CONV // 2 - j
        e = ext if off == 0 else pltpu.roll(ext, off % n_ext, 0)
        term = cw_ref[j:j + 1, :] * e[8:8 + SEQ_BLK]
        acc = term if acc is None else acc + term
    a = _silu(acc)
    for h in range(DN_QK):
        cs = slice(h * DN_HD, (h + 1) * DN_HD)
        qh = a[:, cs]
        q_ref[:, cs] = qh * lax.rsqrt(jnp.sum(qh * qh, axis=-1, keepdims=True) + EPS) * (DN_HD ** -0.5)
        kh = a[:, DN_NQK + h * DN_HD:DN_NQK + (h + 1) * DN_HD]
        k_ref[:, cs] = kh * lax.rsqrt(jnp.sum(kh * kh, axis=-1, keepdims=True) + EPS)
    v_ref[...] = a[:, 2 * DN_NQK:nqkv]

    ba = ba_ref[...]
    lane = lax.broadcasted_iota(jnp.int32, ba.shape, 1)
    row = lax.broadcasted_iota(jnp.int32, ba.shape, 0) % DN_CHUNK
    is_g = ((lane % 16) >= 8) & (lane < 32)
    z = ba + dtb_ref[...]
    softplus = jnp.maximum(z, 0.0) + jnp.log1p(jnp.exp(-jnp.abs(z)))
    val = jnp.where(is_g, -jnp.exp(alog_ref[...]) * softplus, jax.nn.sigmoid(ba))
    cf = val
    cr = val
    s = 1
    while s < DN_CHUNK:
        cf = cf + jnp.where(row >= s, pltpu.roll(cf, s, 0), 0.0)
        cr = cr + jnp.where(row < DN_CHUNK - s, pltpu.roll(cr, SEQ_BLK - s, 0), 0.0)
        s *= 2
    g_ref[...] = jnp.where(lane < 32, jnp.where(is_g, jnp.where(lane < 16, cf, cr), val), 0.0)


def _dn_prep(proj, conv_w, alog_row, dtb_row, tables):
    fb, _, first, last, _ = tables
    nqkv = 2 * DN_NQK + DN_NV
    per = SEQ_BLK // 8
    n8 = N_TOK // 8
    grid_spec = pltpu.PrefetchScalarGridSpec(
        num_scalar_prefetch=3,
        grid=(fb.shape[0],),
        in_specs=[pl.BlockSpec((SEQ_BLK, nqkv), lambda i, blk, *_: (blk[i], 0)),
                  pl.BlockSpec((8, nqkv), lambda i, blk, *_: (jnp.maximum(blk[i] * per - 1, 0), 0)),
                  pl.BlockSpec((8, nqkv), lambda i, blk, *_: (jnp.minimum(blk[i] * per + per, n8 - 1), 0)),
                  pl.BlockSpec((SEQ_BLK, 128), lambda i, blk, *_: (blk[i], (nqkv + DN_NV) // 128)),
                  pl.BlockSpec((DN_CONV, nqkv), lambda i, *_: (0, 0)),
                  pl.BlockSpec((1, 128), lambda i, *_: (0, 0)),
                  pl.BlockSpec((1, 128), lambda i, *_: (0, 0))],
        out_specs=[pl.BlockSpec((SEQ_BLK, DN_NQK), lambda i, blk, *_: (blk[i], 0)),
                   pl.BlockSpec((SEQ_BLK, DN_NQK), lambda i, blk, *_: (blk[i], 0)),
                   pl.BlockSpec((SEQ_BLK, DN_NV), lambda i, blk, *_: (blk[i], 0)),
                   pl.BlockSpec((SEQ_BLK, 128), lambda i, blk, *_: (blk[i], 0))])
    return pl.pallas_call(
        _dn_prep_kernel,
        out_shape=[jax.ShapeDtypeStruct((N_TOK, DN_NQK), F32), jax.ShapeDtypeStruct((N_TOK, DN_NQK), F32),
                   jax.ShapeDtypeStruct((N_TOK, DN_NV), F32), jax.ShapeDtypeStruct((N_TOK, 128), F32)],
        grid_spec=grid_spec,
        compiler_params=_cparams(("arbitrary",)),
        name="dn_prep",
    )(fb, first, last, proj, proj, proj, proj, conv_w, alog_row, dtb_row)


DN_NCH = SEQ_BLK // DN_CHUNK


DN_LEVELS = int(math.log2(DN_CHUNK))
DN_HG = DN_QK


def _dn_setup(d, lane_beta, q, k, kk, qk, v, gates):
    n = SEQ_BLK
    lane = lax.broadcasted_iota(jnp.int32, gates.shape, 1)
    beta = jnp.sum(jnp.where(lane == lane_beta, gates, 0.0), axis=1, keepdims=True)
    gcol = jnp.sum(jnp.where(lane == lane_beta + 8, gates, 0.0), axis=1, keepdims=True)
    ri = lax.broadcasted_iota(jnp.int32, (n, n), 0)
    ci = lax.broadcasted_iota(jnp.int32, (n, n), 1)
    eye = ri == ci
    same = (ri // DN_CHUNK) == (ci // DN_CHUNK)
    grow = jnp.sum(jnp.where(eye, gcol, 0.0), axis=0, keepdims=True)
    if d == 0:
        lower, strict = same & (ri >= ci), same & (ri > ci)
        last_of = (ri // DN_CHUNK) * DN_CHUNK + DN_CHUNK - 1
    else:
        lower, strict = same & (ri <= ci), same & (ri < ci)
        last_of = (ri // DN_CHUNK) * DN_CHUNK
    decay = jnp.exp(jnp.where(lower, gcol - grow, NEG))
    g_last = jnp.sum(jnp.where(ci == last_of, grow, 0.0), axis=1, keepdims=True)
    return dict(
        d=d,
        lmat16=jnp.where(strict, beta * kk * decay, 0.0).astype(BF16),
        rhs16=jnp.concatenate([v * beta, k * (beta * jnp.exp(gcol))], axis=1).astype(BF16),
        aqk16=jnp.where(lower, qk * decay, 0.0).astype(BF16),
        qe16=(q * jnp.exp(gcol)).astype(BF16),
        kd16=(k * jnp.exp(g_last - gcol)).astype(BF16),
        eg=jnp.exp(g_last))


def _dn_level_masks():
    r = np.arange(SEQ_BLK)
    diff = r[:, None] ^ r[None, :]
    level = sum((diff >= (1 << b)).astype(np.int32) for b in range(DN_LEVELS + 1))
    return jnp.asarray(np.stack([level == lv for lv in range(DN_LEVELS + 1)]).astype(np.float32), dtype=BF16)


def _dn_solve_all(chains, lvl_ref):
    ts = [lvl_ref[0] - c["lmat16"] * lvl_ref[1] for c in chains]
    for lv in range(2, DN_LEVELS + 1):
        xs = [_dot(c["lmat16"] * lvl_ref[lv], t) for c, t in zip(chains, ts)]
        ts = [t - _dot(t, x.astype(BF16)).astype(BF16) for t, x in zip(ts, xs)]
    return [_dot(t, c["rhs16"]) for c, t in zip(chains, ts)]


def _dn_scan_all(chains, sols, states):
    us = [s[:, :DN_HD] for s in sols]
    w16 = [s[:, DN_HD:].astype(BF16) for s in sols]
    outs = [[None] * DN_NCH for _ in chains]
    for step in range(DN_NCH):
        idx = [step if c["d"] == 0 else DN_NCH - 1 - step for c in chains]
        rows = [slice(i * DN_CHUNK, (i + 1) * DN_CHUNK) for i in idx]
        s16 = [s.astype(BF16) for s in states]
        vn16 = [(u[r] - _dot(w[r], s)).astype(BF16) for u, w, r, s in zip(us, w16, rows, s16)]
        for n_, (c, r, s, vn) in enumerate(zip(chains, rows, s16, vn16)):
            outs[n_][idx[n_]] = _dot(c["qe16"][r], s) + _dot(c["aqk16"][r, r], vn)
        states = [s * c["eg"][r.start:r.start + 1] + _dot_tn(c["kd16"][r], vn)
                  for s, c, r, vn in zip(states, chains, rows, vn16)]
    return [jnp.concatenate(o, axis=0) for o in outs], states


def _dn_kernel(fb_ref, bb_ref, first_ref, last_ref, sid_ref,
               qf_ref, kf_ref, vf_ref, gf_ref, qb_ref, kb_ref, vb_ref, gb_ref, s0_ref, lvl_ref,
               of_ref, ob_ref, so_ref, s_ref):
    j = pl.program_id(0)
    i = pl.program_id(1)
    is_first = first_ref[i] == 1
    is_lat = sid_ref[i] >= CTX_B

    @pl.when(is_first & is_lat)
    def _():
        s_ref[...] = s0_ref[0]

    @pl.when(is_first & jnp.logical_not(is_lat))
    def _():
        s_ref[...] = jnp.zeros_like(s_ref)

    chains = []
    for d in range(2):
        q_ref, k_ref, v_ref, g_ref = (qf_ref, kf_ref, vf_ref, gf_ref) if d == 0 else (qb_ref, kb_ref, vb_ref, gb_ref)
        gates = g_ref[...]
        for hq in range(DN_HG):
            q = q_ref[:, hq * DN_HD:(hq + 1) * DN_HD]
            k = k_ref[:, hq * DN_HD:(hq + 1) * DN_HD]
            k16 = k.astype(BF16)
            kk = _dot_nt(k16, k16)
            qk = _dot_nt(q.astype(BF16), k16)
            for e in range(DN_REP):
                hv = hq * DN_REP + e
                chains.append(_dn_setup(d, 16 * d + DN_HG * DN_REP * j + hv, q, k, kk, qk,
                                        v_ref[:, hv * DN_HD:(hv + 1) * DN_HD], gates))
    n_hv = DN_HG * DN_REP
    sols = _dn_solve_all(chains, lvl_ref)
    outs, states = _dn_scan_all(chains, sols, [s_ref[d, hv] for d in range(2) for hv in range(n_hv)])
    for n_, (o, s_new) in enumerate(zip(outs, states)):
        d, hv = divmod(n_, n_hv)
        (of_ref if d == 0 else ob_ref)[:, hv * DN_HD:(hv + 1) * DN_HD] = o
        s_ref[d, hv] = s_new
    so_ref[0] = s_ref[...]


def _dn(qn, kn, vv, gates, state, tables):
    n_steps = tables[0].shape[0]
    n_seq = CTX_B + LAT_B

    def fwd(col):
        return lambda j, i, fb, *_: (fb[i], col(j))

    def bwd(col):
        return lambda j, i, fb, bb, *_: (bb[i], col(j))

    head = lambda j: j
    gate = lambda j: 0

    def st_in(j, i, fb, bb, first, last, sid):
        return (jnp.maximum(sid[i] - CTX_B, 0), 0, j, 0, 0)

    def st_out(j, i, fb, bb, first, last, sid):
        return (sid[i], 0, j, 0, 0)

    blk = lambda w: (SEQ_BLK, w)
    qw = DN_HG * DN_HD
    vw = DN_HG * DN_REP * DN_HD
    n_hv = DN_HG * DN_REP
    grid_spec = pltpu.PrefetchScalarGridSpec(
        num_scalar_prefetch=5,
        grid=(DN_QK // DN_HG, n_steps),
        in_specs=[pl.BlockSpec(blk(qw), fwd(head)), pl.BlockSpec(blk(qw), fwd(head)),
                  pl.BlockSpec(blk(vw), fwd(head)), pl.BlockSpec(blk(128), fwd(gate)),
                  pl.BlockSpec(blk(qw), bwd(head)), pl.BlockSpec(blk(qw), bwd(head)),
                  pl.BlockSpec(blk(vw), bwd(head)), pl.BlockSpec(blk(128), bwd(gate)),
                  pl.BlockSpec((1, 2, n_hv, DN_HD, DN_HD), st_in),
                  _resident((DN_LEVELS + 1, SEQ_BLK, SEQ_BLK), lambda j, i, *_: (0, 0, 0))],
        out_specs=[pl.BlockSpec(blk(vw), fwd(head)), pl.BlockSpec(blk(vw), bwd(head)),
                   pl.BlockSpec((1, 2, n_hv, DN_HD, DN_HD), st_out)],
        scratch_shapes=[pltpu.VMEM((2, n_hv, DN_HD, DN_HD), F32)])
    return pl.pallas_call(
        _dn_kernel,
        out_shape=[jax.ShapeDtypeStruct((N_TOK, DN_NV), F32), jax.ShapeDtypeStruct((N_TOK, DN_NV), F32),
                   jax.ShapeDtypeStruct((n_seq, 2, DN_V, DN_HD, DN_HD), F32)],
        grid_spec=grid_spec,
        compiler_params=_cparams(("arbitrary", "arbitrary")),
        name="dn",
    )(*tables, qn, kn, vv, gates, qn, kn, vv, gates, state, _dn_level_masks())


def kernel(x_prompt, x_sample, cache_attn_k, cache_attn_v, state_s5_re, state_s5_im, cache_na_k, cache_na_v,
           state_dn, c, c_ctx, norm_g, w_ada, b_ada, ffn_w_gu, ffn_w_d, a_w_qkv, a_w_o, a_sink,
           s5_lam_re, s5_lam_im, s5_log_dt, s5_b_re, s5_b_im, s5_c_re, s5_c_im, s5_d, s5_w_glu,
           na_w_qkv, na_w_o, na_rpb, dn_w_in, dn_conv_w, dn_w_ba, dn_a_log, dn_dt_bias, dn_out_g, dn_w_o):
    depth = w_ada.shape[0]
    cond8 = jnp.concatenate([c_ctx[None, :], c, jnp.zeros((8 - 1 - LAT_B, D), F32)], axis=0)
    mods = _ada(cond8, w_ada, b_ada)[:, :1 + LAT_B].reshape(depth * (1 + LAT_B), 1, 9 * D)
    ng = norm_g.reshape(depth * 6, 1, D)
    wgu = ffn_w_gu.astype(BF16)
    wd = ffn_w_d.astype(BF16)
    tables = _seq_tables()

    x = _ffn([x_prompt.reshape(N_CTX, D), x_sample.reshape(N_LAT, D)], mods, ng, wgu, wd, 0, 0, head="split")
    qw, kw = A_HEADS * HEAD, A_KV * HEAD
    qkv, k_ctx, v_ctx = _proj(x, mods, ng, a_w_qkv[0].astype(BF16), 0,
                              kv_cols=((qw, qw + kw), (qw + kw, qw + 2 * kw)))
    cos, sin = _rope_tables()
    o_ctx = _attn_ctx(qkv, A_HEADS, A_KV, a_sink[0])
    o_lat = _attn_a_lat(qkv, a_sink[0], cache_attn_k[:, 0].reshape(LAT_B, PAST, kw),
                        cache_attn_v[:, 0].reshape(LAT_B, PAST, kw), cos, sin)
    new_attn_k = k_ctx.reshape(CTX_B, 1, CTX_L, A_KV, HEAD)
    new_attn_v = v_ctx.reshape(CTX_B, 1, CTX_L, A_KV, HEAD)
    x = _ffn([o_ctx, o_lat, a_w_o[0].astype(BF16), x], mods, ng, wgu, wd, 0, 1, head="pair")

    x = _ffn([x], mods, ng, wgu, wd, 1, 0)
    wb, wc, abar = _s5_tables(s5_lam_re[0], s5_lam_im[0], s5_log_dt[0], s5_b_re[0], s5_b_im[0],
                              s5_c_re[0], s5_c_im[0])
    hb = _hmod(x, mods, ng, 1)
    seg = LAT_L // S5_LAT_SEGS
    u_ctx = hb[:N_CTX].reshape(CTX_B, CTX_L, D).transpose(1, 0, 2).reshape(1, N_CTX, D)
    u_lat = hb[N_CTX:].reshape(LAT_B, S5_LAT_SEGS, seg, D).transpose(0, 2, 1, 3).reshape(LAT_B, LAT_L, D)
    h0 = jnp.stack([state_s5_re[:, 0], state_s5_im[:, 0]], axis=2).reshape(LAT_B, 4, S5_N)
    y_ctx, fin = _s5(u_ctx, wb, wc, abar)
    y_lat, = _s5(u_lat, wb, wc, abar, h0)
    y_ctx = y_ctx.reshape(CTX_L, CTX_B, D).transpose(1, 0, 2).reshape(N_CTX, D)
    y_lat = y_lat.reshape(LAT_B, seg, S5_LAT_SEGS, D).transpose(0, 2, 1, 3).reshape(N_LAT, D)
    fin = fin.reshape(2, 2, CTX_B, S5_GROUPS, S5_STATE)
    new_s5_re = jnp.transpose(fin[:, 0], (1, 0, 2, 3))[:, None]
    new_s5_im = jnp.transpose(fin[:, 1], (1, 0, 2, 3))[:, None]
    x = _ffn([y_ctx, y_lat, s5_d, s5_w_glu[0].astype(BF16), x], mods, ng, wgu, wd, 1, 1, head="s5")

    x = _ffn([x], mods, ng, wgu, wd, 2, 0)
    cw = C_HEADS * HEAD
    qkv, k_ctx, v_ctx = _proj(x, mods, ng, na_w_qkv[0].astype(BF16), 2, kv_cols=((cw, 2 * cw), (2 * cw, 3 * cw)))
    o_ctx = _attn_ctx(qkv, C_HEADS, C_HEADS)
    o_lat = _attn_c_lat(qkv, cache_na_k[:, 0].reshape(LAT_B, PAST, cw),
                        cache_na_v[:, 0].reshape(LAT_B, PAST, cw), *_na_tables(na_rpb[0]))
    new_na_k = k_ctx.reshape(CTX_B, 1, CTX_L, C_HEADS, HEAD)
    new_na_v = v_ctx.reshape(CTX_B, 1, CTX_L, C_HEADS, HEAD)
    x = _ffn([o_ctx, o_lat, na_w_o[0].astype(BF16), x], mods, ng, wgu, wd, 2, 1, head="pair")

    x = _ffn([x], mods, ng, wgu, wd, 3, 0)
    w_all = jnp.concatenate([dn_w_in[0], dn_w_ba[0, 0], dn_w_ba[0, 1],
                             jnp.zeros((D, 128 - 4 * DN_V), F32)], axis=1).astype(BF16)
    proj, = _proj(x, mods, ng, w_all, 3)
    pad8 = jnp.zeros((DN_V,), F32)
    gate_row = lambda t: jnp.concatenate([pad8, t[0], pad8, t[1], jnp.zeros((128 - 4 * DN_V,), F32)])[None, :]
    qn, kn, vv, gates = _dn_prep(proj, dn_conv_w[0], gate_row(dn_a_log[0]), gate_row(dn_dt_bias[0]), tables)
    of, ob, fin_dn = _dn(qn, kn, vv, gates, state_dn[:, 0], tables)
    new_dn = fin_dn[:CTX_B][:, None]
    y, z = _ffn([of, ob, proj, dn_out_g, dn_w_o[0].astype(BF16), x], mods, ng, wgu, wd, 3, 1,
                head="dn", split_out=True)
    y = y.reshape(CTX_B, CTX_L, D)
    z = z.reshape(LAT_B, LAT_L, D)
    return (y, z, new_attn_k, new_attn_v, new_s5_re, new_s5_im, new_na_k, new_na_v, new_dn)
```

```python
import functools
import math

import numpy as np
import jax
import jax.numpy as jnp
from jax import lax
from jax.experimental import pallas as pl
from jax.experimental.pallas import tpu as pltpu

F32 = jnp.float32
BF16 = jnp.bfloat16

D = 1024
CTX_B, CTX_L = 16, 256
LAT_B, LAT_L = 2, 4096
N_CTX = CTX_B * CTX_L
N_LAT = LAT_B * LAT_L
N_TOK = N_CTX + N_LAT
PAST = 512
D_FF = 2816
EPS = 1e-6
NEG = -1e30
GRID_W = 64
HEAD = 64
A_HEADS, A_KV = 16, 4
A_WIN = 128
C_HEADS = 16
NA_ROWS, NA_COLS = 8, 16
ROPE_BASE = 10000.0
S5_GROUPS, S5_GROUP, S5_STATE = 64, 16, 64
S5_N = S5_GROUPS * S5_STATE
DN_QK, DN_V, DN_HD, DN_CONV, DN_CHUNK = 4, 8, 128, 5, 64
DN_REP = DN_V // DN_QK
DN_NQK = DN_QK * DN_HD
DN_NV = DN_V * DN_HD
DN_PROJ = 2 * DN_NQK + 2 * DN_NV + 128

TM = 512
SEQ_BLK = 256
VMEM_MB = 56
ADA_TN = 1152


def _cparams(sem, mb=VMEM_MB):
    return pltpu.CompilerParams(dimension_semantics=sem, vmem_limit_bytes=mb * 1024 * 1024)


def _resident(block, index_map):
    return pl.BlockSpec(block, index_map, pipeline_mode=pl.Buffered(1))


def _seg_of_tile(i, tm):
    nct = N_CTX // tm
    return jnp.where(i < nct, 0, 1 + (i - nct) // (LAT_L // tm))


def _dot(a, b):
    return jnp.dot(a, b, preferred_element_type=F32)


def _dot_nt(a, b):
    return lax.dot_general(a, b, (((1,), (1,)), ((), ())), preferred_element_type=F32)


def _dot_tn(a, b):
    return lax.dot_general(a, b, (((0,), (0,)), ((), ())), preferred_element_type=F32)


def _silu(x):
    return x * jax.nn.sigmoid(x)


def _modulate(x, g, m):
    xn = x * lax.rsqrt(jnp.mean(x * x, axis=-1, keepdims=True) + EPS) * g
    return xn * (1.0 + m[:, D:2 * D]) + m[:, :D]


def _residual(x, y, g, m, weight):
    yn = y * lax.rsqrt(jnp.mean(y * y, axis=-1, keepdims=True) + EPS) * g
    return x + weight * m[:, 2 * D:] * yn


def _ada_kernel(cond_ref, w_ref, b_ref, o_ref):
    s = _silu(cond_ref[...]).astype(BF16)
    o_ref[0] = _dot(s, w_ref[0].astype(BF16)) + b_ref[0]


def _ada(cond8, w_ada, b_ada):
    depth, _, n9 = w_ada.shape
    return pl.pallas_call(
        _ada_kernel,
        out_shape=jax.ShapeDtypeStruct((depth, 8, n9), F32),
        grid=(depth, n9 // ADA_TN),
        in_specs=[pl.BlockSpec((8, D), lambda l, j: (0, 0)),
                  pl.BlockSpec((1, D, ADA_TN), lambda l, j: (l, 0, j)),
                  pl.BlockSpec((1, 1, ADA_TN), lambda l, j: (l, 0, j))],
        out_specs=pl.BlockSpec((1, 8, ADA_TN), lambda l, j: (l, 0, j)),
        compiler_params=_cparams(("parallel", "parallel")),
        name="ada",
    )(cond8, w_ada, b_ada.reshape(depth, 1, n9))


FF_CW = 256


def _ffn_input(head, refs, is_ctx):
    if head == "plain":
        return refs[0][...]
    if head == "split":
        return jnp.where(is_ctx, refs[0][...], refs[1][...])
    x_ref, mod_ref, gpost_ref = refs[-3:]
    x, m = x_ref[...], mod_ref[0]
    if head == "pair":
        actx_ref, alat_ref, w_ref = refs[:3]
        a = jnp.where(is_ctx, actx_ref[...], alat_ref[...]).astype(BF16)
        y = _dot(a, w_ref[...])
    elif head == "s5":
        yc_ref, yl_ref, dsk_ref, gpre_ref, w_ref = refs[:5]
        y = jnp.where(is_ctx, yc_ref[...], yl_ref[...]) + dsk_ref[...] * _modulate(x, gpre_ref[0], m)
        t = _dot(jax.nn.gelu(y).astype(BF16), w_ref[...])
        y = t[:, :D] * jax.nn.sigmoid(t[:, D:])
    else:
        of_ref, ob_ref, z_ref, og_ref, w_ref = refs[:5]
        o = of_ref[...] + ob_ref[...]
        og = og_ref[...]
        heads = []
        for h in range(DN_V):
            cs = slice(h * DN_HD, (h + 1) * DN_HD)
            oh = o[:, cs]
            on = oh * lax.rsqrt(jnp.mean(oh * oh, axis=-1, keepdims=True) + EPS) * og
            heads.append((on * _silu(z_ref[:, cs])).astype(BF16))
        y = _dot(jnp.concatenate(heads, axis=1), w_ref[...])
    return _residual(x, y, gpost_ref[0], m, 1.0)


def _ffn_kernel(*refs, head, n_head, n_out):
    mod_ref, gpre_ref, gpost_ref, wgu_ref, wd_ref = refs[n_head:n_head + 5]
    o_refs, act_ref = refs[n_head + 5:n_head + 5 + n_out], refs[-1]
    is_ctx = pl.program_id(0) < N_CTX // TM
    x = _ffn_input(head, refs[:n_head], is_ctx)
    m = mod_ref[0]
    h = _modulate(x, gpre_ref[0], m).astype(BF16)
    for c in range(D_FF // FF_CW):
        g = _dot(h, wgu_ref[:, c * FF_CW:(c + 1) * FF_CW])
        u = _dot(h, wgu_ref[:, D_FF + c * FF_CW:D_FF + (c + 1) * FF_CW])
        act_ref[:, c * FF_CW:(c + 1) * FF_CW] = (_silu(g) * u).astype(BF16)
    y = _dot(act_ref[...], wd_ref[...])
    out = _residual(x, y, gpost_ref[0], m, 0.5)
    if n_out == 1:
        o_refs[0][...] = out
    else:
        @pl.when(is_ctx)
        def _():
            o_refs[0][...] = out

        @pl.when(jnp.logical_not(is_ctx))
        def _():
            o_refs[1][...] = out


def _ffn(xs, mods, ng, wgu, wd, layer, j, head="plain", split_out=False):
    s = 2 * j
    nct = N_CTX // TM
    ctx_spec = pl.BlockSpec((TM, D), lambda i: (jnp.minimum(i, nct - 1), 0))
    lat_spec = pl.BlockSpec((TM, D), lambda i: (jnp.maximum(i - nct, 0), 0))
    all_spec = pl.BlockSpec((TM, D), lambda i: (i, 0))
    row_spec = lambda k: pl.BlockSpec((1, 1, D), lambda i: (layer * 6 + k, 0, 0))
    mod_spec = lambda sub: pl.BlockSpec((1, 1, 3 * D), lambda i: (layer * 3 + _seg_of_tile(i, TM), 0, sub))
    whole = lambda a: _resident(a.shape, lambda i: (0,) * a.ndim)
    if head == "plain":
        head_specs = [all_spec]
    elif head == "split":
        head_specs = [ctx_spec, lat_spec]
    else:
        if head == "pair":
            head_specs = [ctx_spec, lat_spec, whole(xs[2])]
        elif head == "s5":
            head_specs = [ctx_spec, lat_spec, pl.BlockSpec((1, D), lambda i: (0, 0)), row_spec(2), whole(xs[3])]
            xs = xs[:3] + [ng] + xs[3:]
        else:
            zblk = (2 * DN_NQK + DN_NV) // DN_NV
            head_specs = [all_spec, all_spec, pl.BlockSpec((TM, DN_NV), lambda i: (i, zblk)),
                          pl.BlockSpec((1, DN_HD), lambda i: (0, 0)), whole(xs[4])]
        head_specs += [all_spec, mod_spec(1), row_spec(3)]
        xs = xs + [mods, ng]
    if split_out:
        out_shape = [jax.ShapeDtypeStruct((N_CTX, D), F32), jax.ShapeDtypeStruct((N_LAT, D), F32)]
        out_specs = [ctx_spec, lat_spec]
    else:
        out_shape, out_specs = jax.ShapeDtypeStruct((N_TOK, D), F32), all_spec
    return pl.pallas_call(
        functools.partial(_ffn_kernel, head=head, n_head=len(xs), n_out=2 if split_out else 1),
        out_shape=out_shape,
        grid=(N_TOK // TM,),
        in_specs=head_specs
                 + [pl.BlockSpec((1, 1, 3 * D), lambda i: (layer * 3 + _seg_of_tile(i, TM), 0, s)),
                    pl.BlockSpec((1, 1, D), lambda i: (layer * 6 + 2 * s, 0, 0)),
                    pl.BlockSpec((1, 1, D), lambda i: (layer * 6 + 2 * s + 1, 0, 0)),
                    _resident((None, None, D, 2 * D_FF), lambda i: (layer, j, 0, 0)),
                    _resident((None, None, D_FF, D), lambda i: (layer, j, 0, 0))],
        out_specs=out_specs,
        scratch_shapes=[pltpu.VMEM((TM, D_FF), BF16)],
        compiler_params=_cparams(("arbitrary",)),
        name="ffn",
    )(*xs, mods, ng, ng, wgu, wd)


def _proj_kernel(x_ref, mod_ref, g_ref, w_ref, o_ref, *kv_refs, kv_cols):
    h = _modulate(x_ref[...], g_ref[0], mod_ref[0]).astype(BF16)
    p = _dot(h, w_ref[...])
    o_ref[...] = p
    if kv_refs:
        @pl.when(pl.program_id(0) < N_CTX // TM)
        def _():
            for ref, (c0, c1) in zip(kv_refs, kv_cols):
                ref[...] = p[:, c0:c1]


def _proj(x, mods, ng, w, layer, kv_cols=()):
    n = w.shape[1]
    nct = N_CTX // TM
    out_shape = [jax.ShapeDtypeStruct((N_TOK, n), F32)]
    out_specs = [pl.BlockSpec((TM, n), lambda i: (i, 0))]
    for c0, c1 in kv_cols:
        out_shape.append(jax.ShapeDtypeStruct((N_CTX, c1 - c0), F32))
        out_specs.append(pl.BlockSpec((TM, c1 - c0), lambda i: (jnp.minimum(i, nct - 1), 0)))
    return pl.pallas_call(
        functools.partial(_proj_kernel, kv_cols=tuple(kv_cols)),
        out_shape=out_shape,
        grid=(N_TOK // TM,),
        in_specs=[pl.BlockSpec((TM, D), lambda i: (i, 0)),
                  pl.BlockSpec((1, 1, 3 * D), lambda i: (layer * 3 + _seg_of_tile(i, TM), 0, 1)),
                  pl.BlockSpec((1, 1, D), lambda i: (layer * 6 + 2, 0, 0)),
                  _resident((D, n), lambda i: (0, 0))],
        out_specs=out_specs,
        compiler_params=_cparams(("arbitrary",)),
        name="proj",
    )(x, mods, ng, w)


ATT_HG = 8


def _softmax_pv_heads(heads, sinks):
    ms = [functools.reduce(jnp.maximum, [jnp.max(s, axis=-1, keepdims=True) for s, _ in parts]) for parts in heads]
    ms = [m if sk is None else jnp.maximum(m, sk) for m, sk in zip(ms, sinks)]
    ps = [[jnp.exp(s - m) for s, _ in parts] for parts, m in zip(heads, ms)]
    ls = [functools.reduce(jnp.add, [jnp.sum(p, axis=-1, keepdims=True) for p in pp]) for pp in ps]
    ls = [l if sk is None else l + jnp.exp(sk - m) for l, sk, m in zip(ls, sinks, ms)]
    os_ = [functools.reduce(jnp.add, [_dot(p.astype(BF16), v) for p, (_, v) in zip(pp, parts)])
           for pp, parts in zip(ps, heads)]
    return [o / l for o, l in zip(os_, ls)]


def _attn_ctx_kernel(*refs, n_heads, group, has_sink):
    if has_sink:
        sink_ref, q_ref, k_ref, v_ref, o_ref = refs
    else:
        q_ref, k_ref, v_ref, o_ref = refs
    scale = HEAD ** -0.5
    for h0 in range(0, n_heads, ATT_HG):
        heads, sinks = [], []
        for h in range(h0, h0 + ATT_HG):
            kv = h // group
            q = (q_ref[:, h * HEAD:(h + 1) * HEAD] * scale).astype(BF16)
            k = k_ref[:, kv * HEAD:(kv + 1) * HEAD].astype(BF16)
            v = v_ref[:, kv * HEAD:(kv + 1) * HEAD].astype(BF16)
            heads.append([(_dot_nt(q, k), v)])
            sinks.append(sink_ref[h] if has_sink else None)
        for h, o in zip(range(h0, h0 + ATT_HG), _softmax_pv_heads(heads, sinks)):
            o_ref[:, h * HEAD:(h + 1) * HEAD] = o


def _attn_ctx(qkv, n_heads, n_kv, sink=None):
    qw, kw = n_heads * HEAD, n_kv * HEAD
    kern = functools.partial(_attn_ctx_kernel, n_heads=n_heads, group=n_heads // n_kv,
                             has_sink=sink is not None)
    specs = [pl.BlockSpec((CTX_L, qw), lambda b: (b, 0)),
             pl.BlockSpec((CTX_L, kw), lambda b: (b, qw // kw)),
             pl.BlockSpec((CTX_L, kw), lambda b: (b, qw // kw + 1))]
    args = [qkv, qkv, qkv]
    if sink is not None:
        specs = [pl.BlockSpec(memory_space=pltpu.SMEM)] + specs
        args = [sink] + args
    return pl.pallas_call(
        kern,
        out_shape=jax.ShapeDtypeStruct((N_CTX, qw), F32),
        grid=(CTX_B,),
        in_specs=specs,
        out_specs=pl.BlockSpec((CTX_L, qw), lambda b: (b, 0)),
        compiler_params=_cparams(("parallel",)),
        name="attn_ctx",
    )(*args)


def _rope_tables():
    n = HEAD // 4
    inv = ROPE_BASE ** (-jnp.arange(n, dtype=F32) / n)
    t = jnp.arange(LAT_L)
    ang_r = (t // GRID_W).astype(F32)[:, None] * inv[None, :]
    ang_c = (t % GRID_W).astype(F32)[:, None] * inv[None, :]
    cos = jnp.concatenate([jnp.cos(ang_r), jnp.cos(ang_r), jnp.cos(ang_c), jnp.cos(ang_c)], axis=-1)
    sin = jnp.concatenate([-jnp.sin(ang_r), jnp.sin(ang_r), -jnp.sin(ang_c), jnp.sin(ang_c)], axis=-1)
    return jnp.tile(cos, (1, 2)), jnp.tile(sin, (1, 2))


def _rope(x, cos, sin):
    rows = x.shape[0]
    lane = lax.broadcasted_iota(jnp.int32, (rows, 128), 1)
    first = (lane % 32) < 16
    outs = []
    for cb in range(x.shape[1] // 128):
        xb = x[:, cb * 128:(cb + 1) * 128]
        partner = jnp.where(first, pltpu.roll(xb, 112, 1), pltpu.roll(xb, 16, 1))
        outs.append(xb * cos + partner * sin)
    return jnp.concatenate(outs, axis=1)


A_KWIN = 3 * A_WIN


def _attn_a_lat_kernel(sink_ref, q_ref, k_ref, v_ref, kc_ref, vc_ref, cos_ref, sin_ref, o_ref):
    n = pl.program_id(1)
    nb = LAT_L // A_WIN
    start = pl.multiple_of(jnp.clip(n - 1, 0, nb - 3) * A_WIN, A_WIN)
    q0 = pl.multiple_of(n * A_WIN, A_WIN)
    scale = HEAD ** -0.5
    q = _rope(q_ref[...], cos_ref[pl.ds(q0, A_WIN), :], sin_ref[pl.ds(q0, A_WIN), :]) * scale
    k = _rope(k_ref[pl.ds(start, A_KWIN), :], cos_ref[pl.ds(start, A_KWIN), :],
              sin_ref[pl.ds(start, A_KWIN), :]).astype(BF16)
    v = v_ref[pl.ds(start, A_KWIN), :].astype(BF16)
    kc = kc_ref[0].astype(BF16)
    vc = vc_ref[0].astype(BF16)
    qpos = q0 + lax.broadcasted_iota(jnp.int32, (A_WIN, A_KWIN), 0)
    kpos = start + lax.broadcasted_iota(jnp.int32, (A_WIN, A_KWIN), 1)
    ok = jnp.abs(kpos - qpos) <= A_WIN
    group = A_HEADS // A_KV
    for h0 in range(0, A_HEADS, ATT_HG):
        heads, sinks = [], []
        for h in range(h0, h0 + ATT_HG):
            kvs = slice((h // group) * HEAD, (h // group + 1) * HEAD)
            qh = q[:, h * HEAD:(h + 1) * HEAD].astype(BF16)
            s_loc = jnp.where(ok, _dot_nt(qh, k[:, kvs]), NEG)
            s_ctx = _dot_nt(qh, kc[:, kvs])
            heads.append([(s_loc, v[:, kvs]), (s_ctx, vc[:, kvs])])
            sinks.append(sink_ref[h])
        for h, o in zip(range(h0, h0 + ATT_HG), _softmax_pv_heads(heads, sinks)):
            o_ref[:, h * HEAD:(h + 1) * HEAD] = o


def _attn_a_lat(qkv, sink, kc, vc, cos, sin):
    qw, kw = A_HEADS * HEAD, A_KV * HEAD
    nb = LAT_L // A_WIN
    return pl.pallas_call(
        _attn_a_lat_kernel,
        out_shape=jax.ShapeDtypeStruct((N_LAT, qw), F32),
        grid=(LAT_B, nb),
        in_specs=[pl.BlockSpec(memory_space=pltpu.SMEM),
                  pl.BlockSpec((A_WIN, qw), lambda b, n: (N_CTX // A_WIN + b * nb + n, 0)),
                  pl.BlockSpec((LAT_L, kw), lambda b, n: (N_CTX // LAT_L + b, qw // kw)),
                  pl.BlockSpec((LAT_L, kw), lambda b, n: (N_CTX // LAT_L + b, qw // kw + 1)),
                  pl.BlockSpec((1, PAST, kw), lambda b, n: (b, 0, 0)),
                  pl.BlockSpec((1, PAST, kw), lambda b, n: (b, 0, 0)),
                  _resident((LAT_L, 128), lambda b, n: (0, 0)),
                  _resident((LAT_L, 128), lambda b, n: (0, 0))],
        out_specs=pl.BlockSpec((A_WIN, qw), lambda b, n: (b * nb + n, 0)),
        compiler_params=_cparams(("parallel", "parallel")),
        name="attn_a_lat",
    )(sink, qkv, qkv, qkv, kc, vc, cos, sin)


NA_QROWS = 8
NA_QTOK = NA_QROWS * GRID_W
NA_KROWS = NA_QROWS + NA_ROWS
NA_KTOK = NA_KROWS * GRID_W
NA_KBLK = (NA_ROWS // 2) * GRID_W
NA_NKB = NA_KTOK // NA_KBLK
NA_HG = 4


def _na_tables(rpb):
    col = jnp.arange(GRID_W)
    dcol = jnp.clip(col[None, :] - col[:, None], 1 - NA_COLS, NA_COLS - 1) + NA_COLS - 1
    cs = jnp.clip(col - NA_COLS // 2, 0, GRID_W - NA_COLS)
    okc = (col[None, :] >= cs[:, None]) & (col[None, :] < cs[:, None] + NA_COLS)
    onehot = (dcol[None] == jnp.arange(2 * NA_COLS - 1)[:, None, None]).astype(F32)
    tiles = jnp.einsum('hdx,xqk->hdqk', rpb, onehot, precision=lax.Precision.HIGHEST)
    bias = jnp.where(okc[None, None], tiles, NEG)
    a = jnp.arange(NA_QROWS)[:, None]
    j = jnp.arange(NA_KROWS)[None, :]
    rows = LAT_L // GRID_W
    masks = []
    for blk in (0, 1, rows // NA_QROWS - 1):
        r = blk * NA_QROWS + a
        key_row = blk * NA_QROWS - NA_ROWS // 2 + j
        rs = jnp.clip(r - NA_ROWS // 2, 0, rows - NA_ROWS)
        valid = (key_row >= rs) & (key_row < rs + NA_ROWS)
        m = jnp.where(valid, 0.0, NEG).astype(F32)
        masks.append(jnp.broadcast_to(m[:, None, :, None], (NA_QROWS, GRID_W, NA_KROWS, GRID_W))
                     .reshape(NA_QTOK, NA_KTOK))
    return bias, jnp.stack(masks)


def _attn_c_lat_kernel(*refs):
    q_ref = refs[0]
    k_refs = refs[1:1 + NA_NKB]
    v_refs = refs[1 + NA_NKB:1 + 2 * NA_NKB]
    kc_ref, vc_ref, tcol_ref, wm_ref, o_ref, bias_ref = refs[1 + 2 * NA_NKB:]
    i = pl.program_id(2)
    nblk = LAT_L // NA_QTOK

    @pl.when((pl.program_id(1) == 0) & (i == 0))
    def _():
        neg = jnp.full((GRID_W, GRID_W), NEG, F32)
        for e in range(NA_HG):
            for a in range(NA_QROWS):
                tiles = []
                for j in range(NA_KROWS):
                    d = j - NA_ROWS // 2 - a + NA_ROWS - 1
                    tiles.append(tcol_ref[e, d] if 0 <= d < 2 * NA_ROWS - 1 else neg)
                bias_ref[e, a * GRID_W:(a + 1) * GRID_W, :] = jnp.concatenate(tiles, axis=1)

    var = jnp.where(i == 0, 0, jnp.where(i == nblk - 1, 2, 1))
    scale = HEAD ** -0.5
    heads = []
    for e in range(NA_HG):
        hs = slice(e * HEAD, (e + 1) * HEAD)
        qh = (q_ref[:, hs] * scale).astype(BF16)
        parts = []
        for m in range(NA_NKB):
            cs = slice(m * NA_KBLK, (m + 1) * NA_KBLK)
            s = _dot_nt(qh, k_refs[m][:, hs].astype(BF16)) + bias_ref[e, :, cs] + wm_ref[var, :, cs]
            parts.append((s, v_refs[m][:, hs].astype(BF16)))
        parts.append((_dot_nt(qh, kc_ref[0, :, hs].astype(BF16)), vc_ref[0, :, hs].astype(BF16)))
        heads.append(parts)
    for e, o in enumerate(_softmax_pv_heads(heads, [None] * NA_HG)):
        o_ref[:, e * HEAD:(e + 1) * HEAD] = o


def _attn_c_lat(qkv, kc, vc, tcol, wmask):
    qw = C_HEADS * HEAD
    hw = NA_HG * HEAD
    nblk = LAT_L // NA_QTOK
    nkb = LAT_L // NA_KBLK
    per = NA_QTOK // NA_KBLK

    def kv_spec(m, col0):
        def idx(hp, b, i):
            return (N_CTX // NA_KBLK + b * nkb + jnp.clip(i * per - 1 + m, 0, nkb - 1), col0 + hp)
        return pl.BlockSpec((NA_KBLK, hw), idx)

    return pl.pallas_call(
        _attn_c_lat_kernel,
        out_shape=jax.ShapeDtypeStruct((N_LAT, qw), F32),
        grid=(C_HEADS // NA_HG, LAT_B, nblk),
        in_specs=[pl.BlockSpec((NA_QTOK, hw), lambda hp, b, i: (N_CTX // NA_QTOK + b * nblk + i, hp))]
                 + [kv_spec(m, qw // hw) for m in range(NA_NKB)]
                 + [kv_spec(m, 2 * qw // hw) for m in range(NA_NKB)]
                 + [pl.BlockSpec((1, PAST, hw), lambda hp, b, i: (b, 0, hp)),
                    pl.BlockSpec((1, PAST, hw), lambda hp, b, i: (b, 0, hp)),
                    pl.BlockSpec((NA_HG, 2 * NA_ROWS - 1, GRID_W, GRID_W), lambda hp, b, i: (hp, 0, 0, 0)),
                    _resident((3, NA_QTOK, NA_KTOK), lambda hp, b, i: (0, 0, 0))],
        out_specs=pl.BlockSpec((NA_QTOK, hw), lambda hp, b, i: (b * nblk + i, hp)),
        scratch_shapes=[pltpu.VMEM((NA_HG, NA_QTOK, NA_KTOK), F32)],
        compiler_params=_cparams(("arbitrary", "arbitrary", "arbitrary")),
        name="attn_c_lat",
    )(qkv, *([qkv] * (2 * NA_NKB)), kc, vc, tcol, wmask)


def _seq_tables():
    fb, bb, first, last, sid = [], [], [], [], []
    base = 0
    for s, length in enumerate([CTX_L] * CTX_B + [LAT_L] * LAT_B):
        n = length // SEQ_BLK
        for c in range(n):
            fb.append(base + c)
            bb.append(base + n - 1 - c)
            first.append(int(c == 0))
            last.append(int(c == n - 1))
            sid.append(s)
        base += n
    return tuple(jnp.asarray(np.array(t, np.int32)) for t in (fb, bb, first, last, sid))


S5_GB = 8
S5_GW = S5_GB * S5_STATE
S5_ROWS = N_CTX
assert LAT_L == S5_ROWS
S5_LAT_SEGS = 8


def _s5_tables(lam_re, lam_im, log_dt, b_re, b_im, c_re, c_im):
    dt = jnp.exp(log_dt)[..., None]
    lr, li = lam_re * dt, lam_im * dt
    a_re, a_im = jnp.exp(lr) * jnp.cos(li), jnp.exp(lr) * jnp.sin(li)
    den = lam_re * lam_re + lam_im * lam_im
    fr = ((a_re - 1.0) * lam_re + a_im * lam_im) / den
    fi = (a_im * lam_re - (a_re - 1.0) * lam_im) / den
    bb_re = fr[..., None] * b_re - fi[..., None] * b_im
    bb_im = fr[..., None] * b_im + fi[..., None] * b_re
    eye = jnp.eye(S5_GB, dtype=F32)

    def bdiag_in(t):
        t = t.reshape(2, S5_GROUPS // S5_GB, S5_GB, S5_STATE, S5_GROUP)
        return jnp.einsum('dbgpc,gh->dbgchp', t, eye).reshape(2, S5_GROUPS // S5_GB, 128, 512)

    def bdiag_out(t):
        t = t.reshape(2, S5_GROUPS // S5_GB, S5_GB, S5_GROUP, S5_STATE)
        return jnp.einsum('dbgcp,gh->dbgphc', t, eye).reshape(2, S5_GROUPS // S5_GB, 512, 128)

    wb = jnp.stack([bdiag_in(bb_re), bdiag_in(bb_im)], axis=1).astype(BF16)
    wc = jnp.stack([bdiag_out(c_re), bdiag_out(c_im)], axis=1).astype(BF16)

    abar = jnp.stack([a_re.reshape(2, S5_N), a_im.reshape(2, S5_N)], axis=1).reshape(4, S5_N)
    return wb, wc, abar


def _hmod_kernel(x_ref, mod_ref, g_ref, o_ref):
    o_ref[...] = _modulate(x_ref[...], g_ref[0], mod_ref[0]).astype(BF16)


def _hmod(x, mods, ng, layer):
    return pl.pallas_call(
        _hmod_kernel,
        out_shape=jax.ShapeDtypeStruct((N_TOK, D), BF16),
        grid=(N_TOK // TM,),
        in_specs=[pl.BlockSpec((TM, D), lambda i: (i, 0)),
                  pl.BlockSpec((1, 1, 3 * D), lambda i: (layer * 3 + _seg_of_tile(i, TM), 0, 1)),
                  pl.BlockSpec((1, 1, D), lambda i: (layer * 6 + 2, 0, 0))],
        out_specs=pl.BlockSpec((TM, D), lambda i: (i, 0)),
        compiler_params=_cparams(("parallel",)),
        name="hmod",
    )(x, mods, ng)


def _s5_scan(bur, bui, ar, ai, x0, n_seq, steps, d, store):
    def body(k, carry):
        xr, xi = carry
        t = k if d == 0 else steps - 1 - k
        r0 = pl.multiple_of(t * n_seq, n_seq)
        nr = ar * xr - ai * xi + bur[pl.ds(r0, n_seq), :]
        ni = ar * xi + ai * xr + bui[pl.ds(r0, n_seq), :]
        if store:
            bur[pl.ds(r0, n_seq), :] = nr
            bui[pl.ds(r0, n_seq), :] = ni
        return nr, ni

    return lax.fori_loop(0, steps, body, x0, unroll=8)


def _s5_seg_carries(er, ei, h0r, h0i, ar, ai, steps, d):
    pr, pi = ar, ai
    for _ in range(int(math.log2(steps))):
        pr, pi = pr * pr - pi * pi, 2.0 * pr * pi
    row = lax.broadcasted_iota(jnp.int32, er.shape, 0)
    n = S5_LAT_SEGS
    if d == 0:
        cr = jnp.where(row == 0, h0r, pltpu.roll(er, 1, 0))
        ci = jnp.where(row == 0, h0i, pltpu.roll(ei, 1, 0))
    else:
        cr = jnp.where(row == n - 1, h0r, pltpu.roll(er, n - 1, 0))
        ci = jnp.where(row == n - 1, h0i, pltpu.roll(ei, n - 1, 0))
    s = 1
    while s < n:
        keep = (row >= s) if d == 0 else (row + s <= n - 1)
        sh = s if d == 0 else n - s
        sr = jnp.where(keep, pltpu.roll(cr, sh, 0), 0.0)
        si = jnp.where(keep, pltpu.roll(ci, sh, 0), 0.0)
        cr, ci = cr + pr * sr - pi * si, ci + pr * si + pi * sr
        pr, pi = pr * pr - pi * pi, 2.0 * pr * pi
        s *= 2
    return cr, ci


def _s5_kernel(*refs, n_seq):
    if n_seq == S5_LAT_SEGS:
        u_ref, wb_ref, wc_ref, a_ref, h0_ref, y_ref, bur, bui = refs
    else:
        u_ref, wb_ref, wc_ref, a_ref, y_ref, fin_ref, bur, bui = refs
    steps = S5_ROWS // n_seq
    u = u_ref[0]
    y = None
    for d in range(2):
        bur[...] = _dot(u, wb_ref[d, 0, 0])
        bui[...] = _dot(u, wb_ref[d, 1, 0])
        ar = jnp.broadcast_to(a_ref[2 * d:2 * d + 1, :], (n_seq, S5_GW))
        ai = jnp.broadcast_to(a_ref[2 * d + 1:2 * d + 2, :], (n_seq, S5_GW))
        zero = jnp.zeros((n_seq, S5_GW), F32)
        if n_seq == S5_LAT_SEGS:
            er, ei = _s5_scan(bur, bui, ar, ai, (zero, zero), n_seq, steps, d, store=False)
            x0 = _s5_seg_carries(er, ei, h0_ref[0, 2 * d:2 * d + 1, :], h0_ref[0, 2 * d + 1:2 * d + 2, :],
                                 ar, ai, steps, d)
        else:
            x0 = (zero, zero)
        xr, xi = _s5_scan(bur, bui, ar, ai, x0, n_seq, steps, d, store=True)
        if n_seq != S5_LAT_SEGS:
            fin_ref[2 * d] = xr
            fin_ref[2 * d + 1] = xi
        yd = _dot(bur[...].astype(BF16), wc_ref[d, 0, 0]) - _dot(bui[...].astype(BF16), wc_ref[d, 1, 0])
        y = yd if y is None else y + yd
    y_ref[0] = y


def _s5(u_tm, wb, wc, abar, h0=None):
    n_sets = u_tm.shape[0]
    lat = h0 is not None
    n_seq = S5_LAT_SEGS if lat else CTX_B
    in_specs = [pl.BlockSpec((1, S5_ROWS, 128), lambda s, g: (s, 0, g)),
                pl.BlockSpec((2, 2, 1, 128, S5_GW), lambda s, g: (0, 0, g, 0, 0)),
                pl.BlockSpec((2, 2, 1, S5_GW, 128), lambda s, g: (0, 0, g, 0, 0)),
                pl.BlockSpec((4, S5_GW), lambda s, g: (0, g))]
    args = [u_tm, wb, wc, abar]
    out_shape = [jax.ShapeDtypeStruct((n_sets, S5_ROWS, D), F32)]
    out_specs = [pl.BlockSpec((1, S5_ROWS, 128), lambda s, g: (s, 0, g))]
    if lat:
        in_specs.append(pl.BlockSpec((1, 4, S5_GW), lambda s, g: (s, 0, g)))
        args.append(h0)
    else:
        out_shape.append(jax.ShapeDtypeStruct((4, CTX_B, S5_N), F32))
        out_specs.append(pl.BlockSpec((4, CTX_B, S5_GW), lambda s, g: (0, 0, g)))
    return pl.pallas_call(
        functools.partial(_s5_kernel, n_seq=n_seq),
        out_shape=out_shape,
        grid=(n_sets, S5_GROUPS // S5_GB),
        in_specs=in_specs,
        out_specs=out_specs,
        scratch_shapes=[pltpu.VMEM((S5_ROWS, S5_GW), F32), pltpu.VMEM((S5_ROWS, S5_GW), F32)],
        compiler_params=_cparams(("parallel", "parallel")),
        name="s5_lat" if lat else "s5_ctx",
    )(*args)


def _dn_prep_kernel(blk_ref, first_ref, last_ref, x_ref, prev_ref, next_ref, ba_ref, cw_ref, alog_ref, dtb_ref,
                    q_ref, k_ref, v_ref, g_ref):
    i = pl.program_id(0)
    nqkv = 2 * DN_NQK + DN_NV
    x = x_ref[...]
    pv = jnp.where(first_ref[i] == 1, 0.0, prev_ref[...])
    nx = jnp.where(last_ref[i] == 1, 0.0, next_ref[...])
    ext = jnp.concatenate([pv, x, nx], axis=0)
    n_ext = SEQ_BLK + 16
    acc = None
    for j in range(DN_CONV):
        off = DN_CONV // 2 - j
        e = ext if off == 0 else pltpu.roll(ext, off % n_ext, 0)
        term = cw_ref[j:j + 1, :] * e[8:8 + SEQ_BLK]
        acc = term if acc is None else acc + term
    a = _silu(acc)
    for h in range(DN_QK):
        cs = slice(h * DN_HD, (h + 1) * DN_HD)
        qh = a[:, cs]
        q_ref[:, cs] = qh * lax.rsqrt(jnp.sum(qh * qh, axis=-1, keepdims=True) + EPS) * (DN_HD ** -0.5)
        kh = a[:, DN_NQK + h * DN_HD:DN_NQK + (h + 1) * DN_HD]
        k_ref[:, cs] = kh * lax.rsqrt(jnp.sum(kh * kh, axis=-1, keepdims=True) + EPS)
    v_ref[...] = a[:, 2 * DN_NQK:nqkv]

    ba = ba_ref[...]
    lane = lax.broadcasted_iota(jnp.int32, ba.shape, 1)
    row = lax.broadcasted_iota(jnp.int32, ba.shape, 0) % DN_CHUNK
    is_g = ((lane % 16) >= 8) & (lane < 32)
    z = ba + dtb_ref[...]
    softplus = jnp.maximum(z, 0.0) + jnp.log1p(jnp.exp(-jnp.abs(z)))
    val = jnp.where(is_g, -jnp.exp(alog_ref[...]) * softplus, jax.nn.sigmoid(ba))
    cf = val
    cr = val
    s = 1
    while s < DN_CHUNK:
        cf = cf + jnp.where(row >= s, pltpu.roll(cf, s, 0), 0.0)
        cr = cr + jnp.where(row < DN_CHUNK - s, pltpu.roll(cr, SEQ_BLK - s, 0), 0.0)
        s *= 2
    g_ref[...] = jnp.where(lane < 32, jnp.where(is_g, jnp.where(lane < 16, cf, cr), val), 0.0)


def _dn_prep(proj, conv_w, alog_row, dtb_row, tables):
    fb, _, first, last, _ = tables
    nqkv = 2 * DN_NQK + DN_NV
    per = SEQ_BLK // 8
    n8 = N_TOK // 8
    grid_spec = pltpu.PrefetchScalarGridSpec(
        num_scalar_prefetch=3,
        grid=(fb.shape[0],),
        in_specs=[pl.BlockSpec((SEQ_BLK, nqkv), lambda i, blk, *_: (blk[i], 0)),
                  pl.BlockSpec((8, nqkv), lambda i, blk, *_: (jnp.maximum(blk[i] * per - 1, 0), 0)),
                  pl.BlockSpec((8, nqkv), lambda i, blk, *_: (jnp.minimum(blk[i] * per + per, n8 - 1), 0)),
                  pl.BlockSpec((SEQ_BLK, 128), lambda i, blk, *_: (blk[i], (nqkv + DN_NV) // 128)),
                  pl.BlockSpec((DN_CONV, nqkv), lambda i, *_: (0, 0)),
                  pl.BlockSpec((1, 128), lambda i, *_: (0, 0)),
                  pl.BlockSpec((1, 128), lambda i, *_: (0, 0))],
        out_specs=[pl.BlockSpec((SEQ_BLK, DN_NQK), lambda i, blk, *_: (blk[i], 0)),
                   pl.BlockSpec((SEQ_BLK, DN_NQK), lambda i, blk, *_: (blk[i], 0)),
                   pl.BlockSpec((SEQ_BLK, DN_NV), lambda i, blk, *_: (blk[i], 0)),
                   pl.BlockSpec((SEQ_BLK, 128), lambda i, blk, *_: (blk[i], 0))])
    return pl.pallas_call(
        _dn_prep_kernel,
        out_shape=[jax.ShapeDtypeStruct((N_TOK, DN_NQK), F32), jax.ShapeDtypeStruct((N_TOK, DN_NQK), F32),
                   jax.ShapeDtypeStruct((N_TOK, DN_NV), F32), jax.ShapeDtypeStruct((N_TOK, 128), F32)],
        grid_spec=grid_spec,
        compiler_params=_cparams(("arbitrary",)),
        name="dn_prep",
    )(fb, first, last, proj, proj, proj, proj, conv_w, alog_row, dtb_row)


DN_NCH = SEQ_BLK // DN_CHUNK


DN_LEVELS = int(math.log2(DN_CHUNK))
DN_HG = DN_QK


def _dn_setup(d, lane_beta, q, k, kk, qk, v, gates):
    n = SEQ_BLK
    lane = lax.broadcasted_iota(jnp.int32, gates.shape, 1)
    beta = jnp.sum(jnp.where(lane == lane_beta, gates, 0.0), axis=1, keepdims=True)
    gcol = jnp.sum(jnp.where(lane == lane_beta + 8, gates, 0.0), axis=1, keepdims=True)
    ri = lax.broadcasted_iota(jnp.int32, (n, n), 0)
    ci = lax.broadcasted_iota(jnp.int32, (n, n), 1)
    eye = ri == ci
    same = (ri // DN_CHUNK) == (ci // DN_CHUNK)
    grow = jnp.sum(jnp.where(eye, gcol, 0.0), axis=0, keepdims=True)
    if d == 0:
        lower, strict = same & (ri >= ci), same & (ri > ci)
        last_of = (ri // DN_CHUNK) * DN_CHUNK + DN_CHUNK - 1
    else:
        lower, strict = same & (ri <= ci), same & (ri < ci)
        last_of = (ri // DN_CHUNK) * DN_CHUNK
    decay = jnp.exp(jnp.where(lower, gcol - grow, NEG))
    g_last = jnp.sum(jnp.where(ci == last_of, grow, 0.0), axis=1, keepdims=True)
    return dict(
        d=d,
        lmat16=jnp.where(strict, beta * kk * decay, 0.0).astype(BF16),
        rhs16=jnp.concatenate([v * beta, k * (beta * jnp.exp(gcol))], axis=1).astype(BF16),
        aqk16=jnp.where(lower, qk * decay, 0.0).astype(BF16),
        qe16=(q * jnp.exp(gcol)).astype(BF16),
        kd16=(k * jnp.exp(g_last - gcol)).astype(BF16),
        eg=jnp.exp(g_last))


def _dn_level_masks():
    r = np.arange(SEQ_BLK)
    diff = r[:, None] ^ r[None, :]
    level = sum((diff >= (1 << b)).astype(np.int32) for b in range(DN_LEVELS + 1))
    return jnp.asarray(np.stack([level == lv for lv in range(DN_LEVELS + 1)]).astype(np.float32), dtype=BF16)


def _dn_solve_all(chains, lvl_ref):
    ts = [lvl_ref[0] - c["lmat16"] * lvl_ref[1] for c in chains]
    for lv in range(2, DN_LEVELS + 1):
        xs = [_dot(c["lmat16"] * lvl_ref[lv], t) for c, t in zip(chains, ts)]
        ts = [t - _dot(t, x.astype(BF16)).astype(BF16) for t, x in zip(ts, xs)]
    return [_dot(t, c["rhs16"]) for c, t in zip(chains, ts)]


def _dn_scan_all(chains, sols, states):
    us = [s[:, :DN_HD] for s in sols]
    w16 = [s[:, DN_HD:].astype(BF16) for s in sols]
    outs = [[None] * DN_NCH for _ in chains]
    for step in range(DN_NCH):
        idx = [step if c["d"] == 0 else DN_NCH - 1 - step for c in chains]
        rows = [slice(i * DN_CHUNK, (i + 1) * DN_CHUNK) for i in idx]
        s16 = [s.astype(BF16) for s in states]
        vn16 = [(u[r] - _dot(w[r], s)).astype(BF16) for u, w, r, s in zip(us, w16, rows, s16)]
        for n_, (c, r, s, vn) in enumerate(zip(chains, rows, s16, vn16)):
            outs[n_][idx[n_]] = _dot(c["qe16"][r], s) + _dot(c["aqk16"][r, r], vn)
        states = [s * c["eg"][r.start:r.start + 1] + _dot_tn(c["kd16"][r], vn)
                  for s, c, r, vn in zip(states, chains, rows, vn16)]
    return [jnp.concatenate(o, axis=0) for o in outs], states


def _dn_kernel(fb_ref, bb_ref, first_ref, last_ref, sid_ref,
               qf_ref, kf_ref, vf_ref, gf_ref, qb_ref, kb_ref, vb_ref, gb_ref, s0_ref, lvl_ref,
               of_ref, ob_ref, so_ref, s_ref):
    j = pl.program_id(0)
    i = pl.program_id(1)
    is_first = first_ref[i] == 1
    is_lat = sid_ref[i] >= CTX_B

    @pl.when(is_first & is_lat)
    def _():
        s_ref[...] = s0_ref[0]

    @pl.when(is_first & jnp.logical_not(is_lat))
    def _():
        s_ref[...] = jnp.zeros_like(s_ref)

    chains = []
    for d in range(2):
        q_ref, k_ref, v_ref, g_ref = (qf_ref, kf_ref, vf_ref, gf_ref) if d == 0 else (qb_ref, kb_ref, vb_ref, gb_ref)
        gates = g_ref[...]
        for hq in range(DN_HG):
            q = q_ref[:, hq * DN_HD:(hq + 1) * DN_HD]
            k = k_ref[:, hq * DN_HD:(hq + 1) * DN_HD]
            k16 = k.astype(BF16)
            kk = _dot_nt(k16, k16)
            qk = _dot_nt(q.astype(BF16), k16)
            for e in range(DN_REP):
                hv = hq * DN_REP + e
                chains.append(_dn_setup(d, 16 * d + DN_HG * DN_REP * j + hv, q, k, kk, qk,
                                        v_ref[:, hv * DN_HD:(hv + 1) * DN_HD], gates))
    n_hv = DN_HG * DN_REP
    sols = _dn_solve_all(chains, lvl_ref)
    outs, states = _dn_scan_all(chains, sols, [s_ref[d, hv] for d in range(2) for hv in range(n_hv)])
    for n_, (o, s_new) in enumerate(zip(outs, states)):
        d, hv = divmod(n_, n_hv)
        (of_ref if d == 0 else ob_ref)[:, hv * DN_HD:(hv + 1) * DN_HD] = o
        s_ref[d, hv] = s_new
    so_ref[0] = s_ref[...]


def _dn(qn, kn, vv, gates, state, tables):
    n_steps = tables[0].shape[0]
    n_seq = CTX_B + LAT_B

    def fwd(col):
        return lambda j, i, fb, *_: (fb[i], col(j))

    def bwd(col):
        return lambda j, i, fb, bb, *_: (bb[i], col(j))

    head = lambda j: j
    gate = lambda j: 0

    def st_in(j, i, fb, bb, first, last, sid):
        return (jnp.maximum(sid[i] - CTX_B, 0), 0, j, 0, 0)

    def st_out(j, i, fb, bb, first, last, sid):
        return (sid[i], 0, j, 0, 0)

    blk = lambda w: (SEQ_BLK, w)
    qw = DN_HG * DN_HD
    vw = DN_HG * DN_REP * DN_HD
    n_hv = DN_HG * DN_REP
    grid_spec = pltpu.PrefetchScalarGridSpec(
        num_scalar_prefetch=5,
        grid=(DN_QK // DN_HG, n_steps),
        in_specs=[pl.BlockSpec(blk(qw), fwd(head)), pl.BlockSpec(blk(qw), fwd(head)),
                  pl.BlockSpec(blk(vw), fwd(head)), pl.BlockSpec(blk(128), fwd(gate)),
                  pl.BlockSpec(blk(qw), bwd(head)), pl.BlockSpec(blk(qw), bwd(head)),
                  pl.BlockSpec(blk(vw), bwd(head)), pl.BlockSpec(blk(128), bwd(gate)),
                  pl.BlockSpec((1, 2, n_hv, DN_HD, DN_HD), st_in),
                  _resident((DN_LEVELS + 1, SEQ_BLK, SEQ_BLK), lambda j, i, *_: (0, 0, 0))],
        out_specs=[pl.BlockSpec(blk(vw), fwd(head)), pl.BlockSpec(blk(vw), bwd(head)),
                   pl.BlockSpec((1, 2, n_hv, DN_HD, DN_HD), st_out)],
        scratch_shapes=[pltpu.VMEM((2, n_hv, DN_HD, DN_HD), F32)])
    return pl.pallas_call(
        _dn_kernel,
        out_shape=[jax.ShapeDtypeStruct((N_TOK, DN_NV), F32), jax.ShapeDtypeStruct((N_TOK, DN_NV), F32),
                   jax.ShapeDtypeStruct((n_seq, 2, DN_V, DN_HD, DN_HD), F32)],
        grid_spec=grid_spec,
        compiler_params=_cparams(("arbitrary", "arbitrary")),
        name="dn",
    )(*tables, qn, kn, vv, gates, qn, kn, vv, gates, state, _dn_level_masks())


def kernel(x_prompt, x_sample, cache_attn_k, cache_attn_v, state_s5_re, state_s5_im, cache_na_k, cache_na_v,
           state_dn, c, c_ctx, norm_g, w_ada, b_ada, ffn_w_gu, ffn_w_d, a_w_qkv, a_w_o, a_sink,
           s5_lam_re, s5_lam_im, s5_log_dt, s5_b_re, s5_b_im, s5_c_re, s5_c_im, s5_d, s5_w_glu,
           na_w_qkv, na_w_o, na_rpb, dn_w_in, dn_conv_w, dn_w_ba, dn_a_log, dn_dt_bias, dn_out_g, dn_w_o):
    depth = w_ada.shape[0]
    cond8 = jnp.concatenate([c_ctx[None, :], c, jnp.zeros((8 - 1 - LAT_B, D), F32)], axis=0)
    mods = _ada(cond8, w_ada, b_ada)[:, :1 + LAT_B].reshape(depth * (1 + LAT_B), 1, 9 * D)
    ng = norm_g.reshape(depth * 6, 1, D)
    wgu = ffn_w_gu.astype(BF16)
    wd = ffn_w_d.astype(BF16)
    tables = _seq_tables()

    x = _ffn([x_prompt.reshape(N_CTX, D), x_sample.reshape(N_LAT, D)], mods, ng, wgu, wd, 0, 0, head="split")
    qw, kw = A_HEADS * HEAD, A_KV * HEAD
    qkv, k_ctx, v_ctx = _proj(x, mods, ng, a_w_qkv[0].astype(BF16), 0,
                              kv_cols=((qw, qw + kw), (qw + kw, qw + 2 * kw)))
    cos, sin = _rope_tables()
    o_ctx = _attn_ctx(qkv, A_HEADS, A_KV, a_sink[0])
    o_lat = _attn_a_lat(qkv, a_sink[0], cache_attn_k[:, 0].reshape(LAT_B, PAST, kw),
                        cache_attn_v[:, 0].reshape(LAT_B, PAST, kw), cos, sin)
    new_attn_k = k_ctx.reshape(CTX_B, 1, CTX_L, A_KV, HEAD)
    new_attn_v = v_ctx.reshape(CTX_B, 1, CTX_L, A_KV, HEAD)
    x = _ffn([o_ctx, o_lat, a_w_o[0].astype(BF16), x], mods, ng, wgu, wd, 0, 1, head="pair")

    x = _ffn([x], mods, ng, wgu, wd, 1, 0)
    wb, wc, abar = _s5_tables(s5_lam_re[0], s5_lam_im[0], s5_log_dt[0], s5_b_re[0], s5_b_im[0],
                              s5_c_re[0], s5_c_im[0])
    hb = _hmod(x, mods, ng, 1)
    seg = LAT_L // S5_LAT_SEGS
    u_ctx = hb[:N_CTX].reshape(CTX_B, CTX_L, D).transpose(1, 0, 2).reshape(1, N_CTX, D)
    u_lat = hb[N_CTX:].reshape(LAT_B, S5_LAT_SEGS, seg, D).transpose(0, 2, 1, 3).reshape(LAT_B, LAT_L, D)
    h0 = jnp.stack([state_s5_re[:, 0], state_s5_im[:, 0]], axis=2).reshape(LAT_B, 4, S5_N)
    y_ctx, fin = _s5(u_ctx, wb, wc, abar)
    y_lat, = _s5(u_lat, wb, wc, abar, h0)
    y_ctx = y_ctx.reshape(CTX_L, CTX_B, D).transpose(1, 0, 2).reshape(N_CTX, D)
    y_lat = y_lat.reshape(LAT_B, seg, S5_LAT_SEGS, D).transpose(0, 2, 1, 3).reshape(N_LAT, D)
    fin = fin.reshape(2, 2, CTX_B, S5_GROUPS, S5_STATE)
    new_s5_re = jnp.transpose(fin[:, 0], (1, 0, 2, 3))[:, None]
    new_s5_im = jnp.transpose(fin[:, 1], (1, 0, 2, 3))[:, None]
    x = _ffn([y_ctx, y_lat, s5_d, s5_w_glu[0].astype(BF16), x], mods, ng, wgu, wd, 1, 1, head="s5")

    x = _ffn([x], mods, ng, wgu, wd, 2, 0)
    cw = C_HEADS * HEAD
    qkv, k_ctx, v_ctx = _proj(x, mods, ng, na_w_qkv[0].astype(BF16), 2, kv_cols=((cw, 2 * cw), (2 * cw, 3 * cw)))
    o_ctx = _attn_ctx(qkv, C_HEADS, C_HEADS)
    o_lat = _attn_c_lat(qkv, cache_na_k[:, 0].reshape(LAT_B, PAST, cw),
                        cache_na_v[:, 0].reshape(LAT_B, PAST, cw), *_na_tables(na_rpb[0]))
    new_na_k = k_ctx.reshape(CTX_B, 1, CTX_L, C_HEADS, HEAD)
    new_na_v = v_ctx.reshape(CTX_B, 1, CTX_L, C_HEADS, HEAD)
    x = _ffn([o_ctx, o_lat, na_w_o[0].astype(BF16), x], mods, ng, wgu, wd, 2, 1, head="pair")

    x = _ffn([x], mods, ng, wgu, wd, 3, 0)
    w_all = jnp.concatenate([dn_w_in[0], dn_w_ba[0, 0], dn_w_ba[0, 1],
                             jnp.zeros((D, 128 - 4 * DN_V), F32)], axis=1).astype(BF16)
    proj, = _proj(x, mods, ng, w_all, 3)
    pad8 = jnp.zeros((DN_V,), F32)
    gate_row = lambda t: jnp.concatenate([pad8, t[0], pad8, t[1], jnp.zeros((128 - 4 * DN_V,), F32)])[None, :]
    qn, kn, vv, gates = _dn_prep(proj, dn_conv_w[0], gate_row(dn_a_log[0]), gate_row(dn_dt_bias[0]), tables)
    of, ob, fin_dn = _dn(qn, kn, vv, gates, state_dn[:, 0], tables)
    new_dn = fin_dn[:CTX_B][:, None]
    y, z = _ffn([of, ob, proj, dn_out_g, dn_w_o[0].astype(BF16), x], mods, ng, wgu, wd, 3, 1,
                head="dn", split_out=True)
    y = y.reshape(CTX_B, CTX_L, D)
    z = z.reshape(LAT_B, LAT_L, D)
    return (y, z, new_attn_k, new_attn_v, new_s5_re, new_s5_im, new_na_k, new_na_v, new_dn)
```
